```python
import numpy as np
import jax
import jax.numpy as jnp
from jax import lax

D_MODEL = 1024
BATCH = 16
SEQ = 2048
DEPTH = 4

N_MIXERS = 3
HEAD_DIM = 128
ROPE_DIM = HEAD_DIM // 4
ROPE_THETA = 500000.0
NORM_EPS = 1e-6
BLK = 128

A_GROUPS = ((128, 1), (512, 4), (2048, 16))
A_HEADS = D_MODEL // HEAD_DIM
A_WIDTH = A_HEADS * HEAD_DIM
A_IN = 3 * len(A_GROUPS) * A_WIDTH + A_WIDTH

B_HEADS = D_MODEL // HEAD_DIM
B_KV = 2
B_REP = B_HEADS // B_KV
B_WIDTH = B_HEADS * HEAD_DIM
B_KVW = B_KV * HEAD_DIM
CMP_LEN = 32
CMP_STRIDE = 16
CMP_HIDDEN = HEAD_DIM
SEL_LEN = 64
N_SELECT = 16
WIN = 512
SEL_CHUNK = 16
FORCE_SCORE = 1000.0
B_IN = B_WIDTH + 6 * B_KVW + B_WIDTH + 3 * B_HEADS

C_WIDTH = D_MODEL * 5 // 4
C_BLOCKS = 10
C_BLOCK_DIM = C_WIDTH // C_BLOCKS
CONV_W = 4
LRU_C = 8.0

N_A = (DEPTH + 2) // 3
N_B = (DEPTH + 1) // 3
N_C = DEPTH // 3

kernel_name = "hybrid_dilated_nsa_rglru"


def rmsnorm(x, g):
    xf = x.astype(jnp.float32)
    y = xf * lax.rsqrt(jnp.mean(xf * xf, axis=-1, keepdims=True) + NORM_EPS)
    return (y * g.astype(jnp.float32)).astype(x.dtype)


def rope(x):
    S = x.shape[1]
    half = ROPE_DIM // 2
    inv_freq = ROPE_THETA ** (-2.0 * jnp.arange(half, dtype=jnp.float32) / ROPE_DIM)
    ang = jnp.arange(S, dtype=jnp.float32)[:, None] * inv_freq[None, :]
    cos = jnp.cos(ang)[None, :, None, :]
    sin = jnp.sin(ang)[None, :, None, :]
    xf = x.astype(jnp.float32)
    x1 = xf[..., :half]
    x2 = xf[..., half:ROPE_DIM]
    out = jnp.concatenate([x1 * cos - x2 * sin, x2 * cos + x1 * sin, xf[..., ROPE_DIM:]], axis=-1)
    return out.astype(x.dtype)


def masked_softmax(s, mask):
    s = jnp.where(mask, s, -jnp.inf)
    m = jnp.max(s, axis=-1, keepdims=True)
    m = jnp.where(jnp.isfinite(m), m, 0.0)
    e = jnp.exp(s - m)
    den = jnp.sum(e, axis=-1, keepdims=True)
    p = e / jnp.maximum(den, 1e-30)
    lse = (m + jnp.log(den))[..., 0]
    return p, lse


def banded_attention(q, k, v, max_dist):
    B_, L, G, R, Dh = q.shape
    n_prev = -(-max_dist // BLK)
    nb = -(-L // BLK)
    pad = nb * BLK - L
    W = (n_prev + 1) * BLK
    qp = jnp.pad(q, ((0, 0), (0, pad), (0, 0), (0, 0), (0, 0)))
    kp = jnp.pad(k, ((0, 0), (n_prev * BLK, pad), (0, 0), (0, 0)))
    vp = jnp.pad(v, ((0, 0), (n_prev * BLK, pad), (0, 0), (0, 0)))
    qb = qp.reshape(B_, nb, BLK, G, R, Dh)
    kb = kp.reshape(B_, nb + n_prev, BLK, G, Dh)
    vb = vp.reshape(B_, nb + n_prev, BLK, G, Dh)
    kw = jnp.concatenate([kb[:, j:j + nb] for j in range(n_prev + 1)], axis=2)
    vw = jnp.concatenate([vb[:, j:j + nb] for j in range(n_prev + 1)], axis=2)
    s = jnp.einsum('bnqgrd,bnkgd->bngrqk', qb, kw).astype(jnp.float32) * (Dh ** -0.5)
    qpos = jnp.arange(nb)[:, None] * BLK + jnp.arange(BLK)[None, :]
    kpos = jnp.arange(nb)[:, None] * BLK + jnp.arange(W)[None, :] - n_prev * BLK
    dist = qpos[:, :, None] - kpos[:, None, :]
    mask = (dist >= 0) & (dist <= max_dist) & (kpos[:, None, :] >= 0)
    p, lse = masked_softmax(s, mask[None, :, None, None, :, :])
    o = jnp.einsum('bngrqk,bnkgd->bnqgrd', p.astype(vw.dtype), vw)
    o = o.reshape(B_, nb * BLK, G, R, Dh)[:, :L]
    lse = jnp.transpose(lse, (0, 1, 4, 2, 3)).reshape(B_, nb * BLK, G, R)[:, :L]
    return o, lse


def dilate(t, d):
    B_, S = t.shape[:2]
    t = t.reshape(B_, S // d, d, *t.shape[2:])
    t = jnp.moveaxis(t, 2, 1)
    return t.reshape(B_ * d, S // d, *t.shape[3:])


def undilate(t, d, B_):
    L = t.shape[1]
    t = t.reshape(B_, d, L, *t.shape[2:])
    t = jnp.moveaxis(t, 1, 2)
    return t.reshape(B_, L * d, *t.shape[3:])


def mixer_a(xn, w_in, w_out):
    B_, S, _ = xn.shape
    n_g = len(A_GROUPS)
    u = xn @ w_in
    qkv = u[..., :3 * n_g * A_WIDTH].reshape(B_, S, n_g, 3, A_HEADS, HEAD_DIM)
    z = u[..., 3 * n_g * A_WIDTH:]
    outs, lses = [], []
    for g, (win, dil) in enumerate(A_GROUPS):
        q = dilate(rope(qkv[:, :, g, 0]), dil)[:, :, :, None, :]
        k = dilate(rope(qkv[:, :, g, 1]), dil)
        v = dilate(qkv[:, :, g, 2], dil)
        o, lse = banded_attention(q, k, v, win // dil)
        outs.append(undilate(o[:, :, :, 0], dil, B_))
        lses.append(undilate(lse[:, :, :, 0], dil, B_))
    alpha = jax.nn.softmax(jnp.stack(lses), axis=0)
    o = jnp.sum(alpha[..., None] * jnp.stack(outs).astype(jnp.float32), axis=0)
    y = o.reshape(B_, S, A_WIDTH).astype(xn.dtype) * jax.nn.silu(z)
    return y @ w_out


def compress(k, pe, w1, w2):
    B_, S, G, Dh = k.shape
    ch = k.reshape(B_, S // CMP_STRIDE, CMP_STRIDE, G, Dh)
    blocks = jnp.concatenate([ch[:, :-1], ch[:, 1:]], axis=2)
    h = jax.nn.silu(jnp.einsum('bjpgd,pde->bjge', blocks + pe[:, None, :], w1))
    return jnp.einsum('bjge,ef->bjgf', h, w2)


def block_cover(S):
    n_cmp = S // CMP_STRIDE - 1
    n_slc = S // SEL_LEN
    j = np.arange(n_cmp)[:, None]
    s = np.arange(n_slc)[None, :]
    cover = (j * CMP_STRIDE < (s + 1) * SEL_LEN) & (j * CMP_STRIDE + CMP_LEN > s * SEL_LEN)
    return jnp.asarray(cover.astype(np.float32))


def selected_attention(q, k, v, idx, valid):
    B_, S, G, R, Dh = q.shape
    n_slc = S // SEL_LEN
    K = idx.shape[-1]
    nc = S // SEL_CHUNK
    kb = jnp.transpose(k.reshape(B_, n_slc, SEL_LEN, G, Dh), (0, 3, 1, 2, 4))
    vb = jnp.transpose(v.reshape(B_, n_slc, SEL_LEN, G, Dh), (0, 3, 1, 2, 4))
    qc = jnp.transpose(q.reshape(B_, nc, SEL_CHUNK, G, R, Dh), (1, 0, 3, 4, 2, 5))
    ic = jnp.transpose(idx.reshape(B_, G, nc, SEL_CHUNK, K), (2, 0, 1, 3, 4))
    vc = jnp.transpose(valid.reshape(B_, G, nc, SEL_CHUNK, K), (2, 0, 1, 3, 4))
    tc = jnp.arange(S).reshape(nc, SEL_CHUNK)
    bi = jnp.arange(B_)[:, None, None, None]
    gi = jnp.arange(G)[None, :, None, None]

    def one_chunk(args):
        qx, ix, vx, tx = args
        gk = kb[bi, gi, ix]
        gv = vb[bi, gi, ix]
        s = jnp.einsum('bgrcd,bgckld->bgrckl', qx, gk).astype(jnp.float32) * (Dh ** -0.5)
        tok = ix[..., None] * SEL_LEN + jnp.arange(SEL_LEN)
        mask = (tok <= tx[None, None, :, None, None]) & vx[..., None]
        s = s.reshape(B_, G, R, SEL_CHUNK, K * SEL_LEN)
        mask = mask[:, :, None].reshape(B_, G, 1, SEL_CHUNK, K * SEL_LEN)
        p, _ = masked_softmax(s, mask)
        return jnp.einsum('bgrcn,bgcnd->bgrcd', p.astype(gv.dtype),
                          gv.reshape(B_, G, SEL_CHUNK, K * SEL_LEN, Dh))

    o = lax.map(one_chunk, (qc, ic, vc, tc))
    return jnp.transpose(o, (1, 0, 4, 2, 3, 5)).reshape(B_, S, G, R, Dh)


def mixer_b(xn, w_in, gate_b, pe_k, w1_k, w2_k, pe_v, w1_v, w2_v, w_out):
    B_, S, _ = xn.shape
    u = xn @ w_in
    o0 = B_WIDTH
    o1 = o0 + 6 * B_KVW
    o2 = o1 + B_WIDTH
    q = rope(u[..., :o0].reshape(B_, S, B_HEADS, HEAD_DIM)).reshape(B_, S, B_KV, B_REP, HEAD_DIM)
    kv = u[..., o0:o1].reshape(B_, S, 6, B_KV, HEAD_DIM)
    k_cmp, v_cmp = rope(kv[:, :, 0]), kv[:, :, 1]
    k_sel, v_sel = rope(kv[:, :, 2]), kv[:, :, 3]
    k_win, v_win = rope(kv[:, :, 4]), kv[:, :, 5]
    z = u[..., o1:o2]
    gates = jax.nn.sigmoid((u[..., o2:] + gate_b).astype(jnp.float32)).reshape(B_, S, B_KV, B_REP, 3)

    kc = compress(k_cmp, pe_k, w1_k, w2_k)
    vc = compress(v_cmp, pe_v, w1_v, w2_v)
    n_cmp = kc.shape[1]
    t = jnp.arange(S)
    blk_end = jnp.arange(n_cmp) * CMP_STRIDE + CMP_LEN - 1
    s = jnp.einsum('bsgrd,bjgd->bgrsj', q, kc).astype(jnp.float32) * (HEAD_DIM ** -0.5)
    p_cmp, _ = masked_softmax(s, blk_end[None, :] <= t[:, None])
    o_cmp = jnp.einsum('bgrsj,bjgd->bsgrd', p_cmp.astype(vc.dtype), vc)

    n_slc = S // SEL_LEN
    p_slc = jnp.einsum('bgrsj,jn->bgsn', p_cmp, block_cover(S))
    blk = jnp.arange(n_slc)[None, :]
    cur = (t // SEL_LEN)[:, None]
    forced = (blk == 0) | (blk == cur) | (blk == cur - 1)
    score = jnp.where(blk <= cur, jnp.where(forced, FORCE_SCORE, p_slc), -jnp.inf)
    vals, idx = lax.top_k(score, min(N_SELECT, n_slc))
    o_sel = selected_attention(q, k_sel, v_sel, idx, jnp.isfinite(vals))

    o_win, _ = banded_attention(q, k_win, v_win, WIN - 1)

    o = gates[..., 0:1] * o_cmp + gates[..., 1:2] * o_sel + gates[..., 2:3] * o_win
    y = o.reshape(B_, S, B_WIDTH).astype(xn.dtype) * jax.nn.silu(z)
    return y @ w_out


def mixer_c(xn, w_in, conv_w, conv_b, wa, ba, wx, bx, lam, w_out):
    B_, S, _ = xn.shape
    u = xn @ w_in
    xb = u[..., :C_WIDTH]
    z = u[..., C_WIDTH:]
    xc = lax.conv_general_dilated(xb, conv_w[:, None, :], window_strides=(1,),
                                  padding=[(CONV_W - 1, 0)],
                                  dimension_numbers=('NWC', 'WIO', 'NWC'),
                                  feature_group_count=C_WIDTH) + conv_b
    xr = xc.reshape(B_, S, C_BLOCKS, C_BLOCK_DIM)
    r = jax.nn.sigmoid((jnp.einsum('bsnc,ncd->bsnd', xr, wa).reshape(B_, S, C_WIDTH) + ba).astype(jnp.float32))
    i = jax.nn.sigmoid((jnp.einsum('bsnc,ncd->bsnd', xr, wx).reshape(B_, S, C_WIDTH) + bx).astype(jnp.float32))
    log_a = -LRU_C * r * jax.nn.softplus(-lam.astype(jnp.float32))
    a = jnp.exp(log_a)
    b = jnp.sqrt(-jnp.expm1(2.0 * log_a)) * i * xc.astype(jnp.float32)

    def combine(left, right):
        a1, b1 = left
        a2, b2 = right
        return a1 * a2, a2 * b1 + b2

    _, h = lax.associative_scan(combine, (a, b), axis=1)
    y = h.astype(xn.dtype) * jax.nn.silu(z)
    return y @ w_out


def _normal(key, shape, scale):
    return scale * jax.random.normal(key, shape, dtype=jnp.float32)


def setup_inputs(seed: int = 0) -> dict:
    key = jax.random.key(seed)
    ks = jax.random.split(key, 24)
    hd = HEAD_DIM
    a0 = jax.random.uniform(ks[22], (N_C, C_WIDTH), dtype=jnp.float32, minval=0.9, maxval=0.999)
    p = a0 ** (1.0 / LRU_C)
    return {
        'x': _normal(ks[0], (BATCH, SEQ, D_MODEL), 1.0),
        'norm_g': 1.0 + _normal(ks[1], (DEPTH, D_MODEL), 0.1),
        'final_g': 1.0 + _normal(ks[2], (D_MODEL,), 0.1),
        'a_w_in': _normal(ks[3], (N_A, D_MODEL, A_IN), D_MODEL ** -0.5),
        'a_w_out': _normal(ks[4], (N_A, A_WIDTH, D_MODEL), A_WIDTH ** -0.5),
        'b_w_in': _normal(ks[5], (N_B, D_MODEL, B_IN), D_MODEL ** -0.5),
        'b_gate_b': _normal(ks[6], (N_B, 3 * B_HEADS), 0.1),
        'b_pe_k': _normal(ks[7], (N_B, CMP_LEN, hd), 0.5),
        'b_w1_k': _normal(ks[8], (N_B, CMP_LEN, hd, CMP_HIDDEN), (CMP_LEN * hd) ** -0.5),
        'b_w2_k': _normal(ks[9], (N_B, CMP_HIDDEN, hd), CMP_HIDDEN ** -0.5),
        'b_pe_v': _normal(ks[10], (N_B, CMP_LEN, hd), 0.5),
        'b_w1_v': _normal(ks[11], (N_B, CMP_LEN, hd, CMP_HIDDEN), (CMP_LEN * hd) ** -0.5),
        'b_w2_v': _normal(ks[12], (N_B, CMP_HIDDEN, hd), CMP_HIDDEN ** -0.5),
        'b_w_out': _normal(ks[13], (N_B, B_WIDTH, D_MODEL), B_WIDTH ** -0.5),
        'c_w_in': _normal(ks[14], (N_C, D_MODEL, 2 * C_WIDTH), D_MODEL ** -0.5),
        'c_conv_w': _normal(ks[15], (N_C, CONV_W, C_WIDTH), CONV_W ** -0.5),
        'c_conv_b': _normal(ks[16], (N_C, C_WIDTH), 0.02),
        'c_wa': _normal(ks[17], (N_C, C_BLOCKS, C_BLOCK_DIM, C_BLOCK_DIM), C_BLOCK_DIM ** -0.5),
        'c_ba': _normal(ks[18], (N_C, C_WIDTH), 0.1),
        'c_wx': _normal(ks[19], (N_C, C_BLOCKS, C_BLOCK_DIM, C_BLOCK_DIM), C_BLOCK_DIM ** -0.5),
        'c_bx': _normal(ks[20], (N_C, C_WIDTH), 0.1),
        'c_lambda': jnp.log(p) - jnp.log1p(-p),
        'c_w_out': _normal(ks[21], (N_C, C_WIDTH, D_MODEL), C_WIDTH ** -0.5),
    }


def reference(x, norm_g, final_g, a_w_in, a_w_out, b_w_in, b_gate_b, b_pe_k, b_w1_k, b_w2_k,
              b_pe_v, b_w1_v, b_w2_v, b_w_out, c_w_in, c_conv_w, c_conv_b, c_wa, c_ba,
              c_wx, c_bx, c_lambda, c_w_out):
    for i in range(DEPTH):
        kind = i % N_MIXERS
        j = i // N_MIXERS
        h = rmsnorm(x, norm_g[i])
        if kind == 0:
            y = mixer_a(h, a_w_in[j], a_w_out[j])
        elif kind == 1:
            y = mixer_b(h, b_w_in[j], b_gate_b[j], b_pe_k[j], b_w1_k[j], b_w2_k[j],
                        b_pe_v[j], b_w1_v[j], b_w2_v[j], b_w_out[j])
        else:
            y = mixer_c(h, c_w_in[j], c_conv_w[j], c_conv_b[j], c_wa[j], c_ba[j],
                        c_wx[j], c_bx[j], c_lambda[j], c_w_out[j])
        x = x + y
    return rmsnorm(x, final_g)
```

```python
import functools

import numpy as np
import jax
import jax.numpy as jnp
from jax import lax
from jax.experimental import pallas as pl
from jax.experimental.pallas import tpu as pltpu

F32 = jnp.float32
BF16 = jnp.bfloat16

HEAD_DIM = 128
LANES = 128
ROPE_DIM = HEAD_DIM // 4
ROPE_HALF = ROPE_DIM // 2
ROPE_THETA = 500000.0
NORM_EPS = 1e-6
SCALE = HEAD_DIM ** -0.5
NEG = -1e30
VMEM_LIMIT = 48 * 1024 * 1024

A_GROUPS = ((128, 1), (512, 4), (2048, 16))
A_BLK = 128
B_KV = 2
B_REP = 4
CMP_LEN = 32
CMP_STRIDE = 16
SEL_LEN = 64
N_SELECT = 16
WIN = 512
FORCE_SCORE = 1000.0
B_TQ = 128
C_BLOCKS = 10
CONV_W = 4
LRU_C = 8.0
C_CHUNK = 256


def _cparams(*sem):
    return pltpu.CompilerParams(dimension_semantics=sem, vmem_limit_bytes=VMEM_LIMIT)


def _sigmoid(x):
    return 1.0 / (1.0 + jnp.exp(-x))


def _silu(x):
    return x * _sigmoid(x)


def _rope_tables(S):
    inv_freq = ROPE_THETA ** (-2.0 * jnp.arange(ROPE_HALF, dtype=F32) / ROPE_DIM)
    ang = jnp.arange(S, dtype=F32)[:, None] * inv_freq[None, :]
    cos, sin = jnp.cos(ang), jnp.sin(ang)
    ones = jnp.ones((S, HEAD_DIM - ROPE_DIM), F32)
    zeros_h = jnp.zeros((S, ROPE_HALF), F32)
    zeros_r = jnp.zeros((S, HEAD_DIM - ROPE_DIM), F32)
    c = jnp.concatenate([cos, cos, ones], axis=1)
    sa = jnp.concatenate([-sin, zeros_h, zeros_r], axis=1)
    sb = jnp.concatenate([zeros_h, sin, zeros_r], axis=1)
    return c, sa, sb


def _norm_proj_kernel(*refs, nb, hs, lt, n_rope_tiles):
    if n_rope_tiles:
        x_ref, g_ref, w_ref, c_ref, sa_ref, sb_ref, o_ref, xn_ref = refs
    else:
        x_ref, g_ref, w_ref, o_ref, xn_ref = refs
    j = pl.program_id(2)

    @pl.when(j == 0)
    def _():
        xf = x_ref[...]
        ms = jnp.mean(xf * xf, axis=-1, keepdims=True)
        xn_ref[...] = (xf * lax.rsqrt(ms + NORM_EPS) * g_ref[...]).astype(BF16)

    res = jnp.dot(xn_ref[...], w_ref[...], preferred_element_type=F32)

    def store(rope):
        for bb in range(nb):
            for hh in range(hs):
                sub = res[bb * lt:(bb + 1) * lt, hh * LANES:(hh + 1) * LANES]
                if rope:
                    sub = (sub * c_ref[...]
                           + pltpu.roll(sub, LANES - ROPE_HALF, 1) * sa_ref[...]
                           + pltpu.roll(sub, ROPE_HALF, 1) * sb_ref[...])
                o_ref[0, bb, hh] = sub.astype(o_ref.dtype)

    if n_rope_tiles:
        pl.when(j < n_rope_tiles)(lambda: store(True))
        pl.when(j >= n_rope_tiles)(lambda: store(False))
    else:
        store(False)


def _norm_proj(x, gain, w, *, d, n_rope, out_dtype, tables=None, tn=512):
    B, S, D = x.shape
    N = w.shape[1]
    L = S // d
    rows = B * L
    tm = min(1024, rows)
    hs = tn // LANES
    assert N % tn == 0 and rows % tm == 0 and n_rope % hs == 0
    assert (tm % L == 0) or (L % tm == 0)
    n_rope_tiles = n_rope // hs
    if tm >= L:
        nb, lt = tm // L, L
        o_spec = pl.BlockSpec((1, nb, hs, L, LANES), lambda r, mi, j: (r, mi, j, 0, 0))
        t_map = lambda r, mi, j: (0, r)
    else:
        nb, lt, tps = 1, tm, L // tm
        o_spec = pl.BlockSpec((1, 1, hs, tm, LANES), lambda r, mi, j: (r, mi // tps, j, mi % tps, 0))
        t_map = lambda r, mi, j: (mi % tps, r)
    in_specs = [
        pl.BlockSpec((tm, D), lambda r, mi, j: (mi, r)),
        pl.BlockSpec((1, D), lambda r, mi, j: (0, 0)),
        pl.BlockSpec((D, tn), lambda r, mi, j: (0, j)),
    ]
    args = [x.reshape(rows, d * D), gain.reshape(1, D), w]
    if n_rope_tiles:
        in_specs += [pl.BlockSpec((lt, LANES), t_map)] * 3
        args += [t.reshape(L, d * LANES) for t in tables]
    return pl.pallas_call(
        functools.partial(_norm_proj_kernel, nb=nb, hs=hs, lt=lt, n_rope_tiles=n_rope_tiles),
        grid=(d, rows // tm, N // tn),
        in_specs=in_specs,
        out_specs=o_spec,
        out_shape=jax.ShapeDtypeStruct((d, B, N // LANES, L, LANES), out_dtype),
        scratch_shapes=[pltpu.VMEM((tm, D), BF16)],
        compiler_params=_cparams("parallel", "parallel", "arbitrary"),
    )(*args)


def _out_proj_kernel(*refs, nh, final):
    if final:
        y_ref, w_ref, x_ref, g_ref, o_ref = refs
    else:
        y_ref, w_ref, x_ref, o_ref = refs
    y = jnp.concatenate([y_ref[0, h] for h in range(nh)], axis=1)
    xn = x_ref[0] + jnp.dot(y, w_ref[...], preferred_element_type=F32)
    if final:
        ms = jnp.mean(xn * xn, axis=-1, keepdims=True)
        xn = xn * lax.rsqrt(ms + NORM_EPS) * g_ref[...]
    o_ref[0] = xn


def _out_proj(y, w, x, final_gain=None, tm=512):
    B, nh, S, _ = y.shape
    D = x.shape[-1]
    final = final_gain is not None
    in_specs = [
        pl.BlockSpec((1, nh, tm, LANES), lambda b, i: (b, 0, i, 0)),
        pl.BlockSpec((nh * LANES, D), lambda b, i: (0, 0)),
        pl.BlockSpec((1, tm, D), lambda b, i: (b, i, 0)),
    ]
    args = [y, w, x]
    if final:
        in_specs.append(pl.BlockSpec((1, D), lambda b, i: (0, 0)))
        args.append(final_gain.reshape(1, D))
    return pl.pallas_call(
        functools.partial(_out_proj_kernel, nh=nh, final=final),
        grid=(B, S // tm),
        in_specs=in_specs,
        out_specs=pl.BlockSpec((1, tm, D), lambda b, i: (b, i, 0)),
        out_shape=jax.ShapeDtypeStruct((B, S, D), F32),
        compiler_params=_cparams("parallel", "parallel"),
    )(*args)


def _attn_a_kernel(q1, k1, v1, q2, k2, v2, q3, k3, v3, z_ref, y_ref, o_s, l_s, *, S):
    groups = ((q1, k1, v1), (q2, k2, v2), (q3, k3, v3))
    nblk = S // A_BLK
    for gi, ((q_ref, k_ref, v_ref), (win, d)) in enumerate(zip(groups, A_GROUPS)):
        L = S // d
        nqb = L // A_BLK
        nk = 2 * A_BLK if nqb > 1 else A_BLK
        max_dist = win // d

        def blk(t, carry, q_ref=q_ref, k_ref=k_ref, v_ref=v_ref, d=d, nqb=nqb, nk=nk,
                max_dist=max_dist, gi=gi):
            r = t // nqb
            i = t % nqb
            ks = jnp.maximum(i - 1, 0) * A_BLK
            q = q_ref[r, 0, 0, pl.ds(pl.multiple_of(i * A_BLK, A_BLK), A_BLK), :]
            k = k_ref[r, 0, 0, pl.ds(pl.multiple_of(ks, A_BLK), nk), :]
            v = v_ref[r, 0, 0, pl.ds(pl.multiple_of(ks, A_BLK), nk), :]
            s = lax.dot_general(q, k, (((1,), (1,)), ((), ())), preferred_element_type=F32) * SCALE
            qpos = i * A_BLK + lax.broadcasted_iota(jnp.int32, (A_BLK, nk), 0)
            kpos = ks + lax.broadcasted_iota(jnp.int32, (A_BLK, nk), 1)
            dist = qpos - kpos
            s = jnp.where((dist >= 0) & (dist <= max_dist), s, NEG)
            m = jnp.max(s, axis=1, keepdims=True)
            p = jnp.exp(s - m)
            l = jnp.sum(p, axis=1, keepdims=True)
            o = jnp.dot(p.astype(BF16), v, preferred_element_type=F32) / l
            lse = jnp.broadcast_to(m + jnp.log(l), (A_BLK, LANES))
            if d == 1:
                rows = pl.ds(pl.multiple_of(t * A_BLK, A_BLK), A_BLK)
            else:
                rows = pl.ds(i * (A_BLK * d) + r, A_BLK, stride=d)
            o_s[gi, rows, :] = o
            l_s[gi, rows, :] = lse
            return carry

        lax.fori_loop(0, nblk, blk, 0)

    def merge(c, carry):
        rows = pl.ds(pl.multiple_of(c * A_BLK, A_BLK), A_BLK)
        l1, l2, l3 = l_s[0, rows, :], l_s[1, rows, :], l_s[2, rows, :]
        mx = jnp.maximum(jnp.maximum(l1, l2), l3)
        e1, e2, e3 = jnp.exp(l1 - mx), jnp.exp(l2 - mx), jnp.exp(l3 - mx)
        o = (e1 * o_s[0, rows, :] + e2 * o_s[1, rows, :] + e3 * o_s[2, rows, :]) / (e1 + e2 + e3)
        y_ref[0, 0, rows, :] = (o * _silu(z_ref[0, 0, 0, rows, :])).astype(BF16)
        return carry

    lax.fori_loop(0, nblk, merge, 0)


def _attn_a(qkv, z):
    _, B, nh, S, _ = z.shape
    in_specs, args = [], []
    for arr, (_, d) in zip(qkv, A_GROUPS):
        L = S // d
        for t in range(3):
            in_specs.append(pl.BlockSpec((d, 1, 1, L, LANES), lambda b, h, t=t, nh=nh: (0, b, t * nh + h, 0, 0)))
            args.append(arr)
    in_specs.append(pl.BlockSpec((1, 1, 1, S, LANES), lambda b, h: (0, b, h, 0, 0)))
    args.append(z)
    return pl.pallas_call(
        functools.partial(_attn_a_kernel, S=S),
        grid=(B, nh),
        in_specs=in_specs,
        out_specs=pl.BlockSpec((1, 1, S, LANES), lambda b, h: (b, h, 0, 0)),
        out_shape=jax.ShapeDtypeStruct((B, nh, S, LANES), BF16),
        scratch_shapes=[pltpu.VMEM((3, S, LANES), F32), pltpu.VMEM((3, S, LANES), F32)],
        compiler_params=_cparams("parallel", "parallel"),
    )(*args)


def _mixer_a(x, gain, w_in, w_out, tables, final_gain=None):
    B, S, D = x.shape
    W = w_in.astype(BF16)
    n_g = len(A_GROUPS)
    aw = D
    qkv = []
    for g, (_, d) in enumerate(A_GROUPS):
        qkv.append(_norm_proj(x, gain, W[:, g * 3 * aw:(g + 1) * 3 * aw], d=d, n_rope=2 * aw // LANES,
                              out_dtype=BF16, tables=tables))
    z = _norm_proj(x, gain, W[:, n_g * 3 * aw:], d=1, n_rope=0, out_dtype=F32)
    y = _attn_a(qkv, z)
    return _out_proj(y, w_out.astype(BF16), x, final_gain)


def _compress_kernel(k_ref, v_ref, pek_ref, pev_ref, w1k_ref, w1v_ref, w2k_ref, w2v_ref, o_ref):
    def one(c_ref, pe_ref, w1_ref, w2_ref):
        half = c_ref.shape[-1]
        ch = c_ref[0, 0]
        u = jnp.dot((ch + pe_ref[0:1, :]).astype(BF16), w1_ref[0:half, :], preferred_element_type=F32)
        v = jnp.dot((ch + pe_ref[1:2, :]).astype(BF16), w1_ref[half:2 * half, :], preferred_element_type=F32)
        h = _silu(u + pltpu.roll(v, v.shape[0] - 1, 0))
        return jnp.dot(h.astype(BF16), w2_ref[...], preferred_element_type=F32).astype(BF16)

    o_ref[0, 0, 0] = one(k_ref, pek_ref, w1k_ref, w2k_ref)
    o_ref[0, 1, 0] = one(v_ref, pev_ref, w1v_ref, w2v_ref)


def _compress(cmp_arr, pe_k, w1_k, w2_k, pe_v, w1_v, w2_v):
    B, ns, S, _ = cmp_arr.shape
    nch = S // CMP_STRIDE
    cw = CMP_STRIDE * HEAD_DIM
    chunks = cmp_arr.reshape(B, ns, nch, cw)
    pe = lambda p: p.reshape(2, cw)
    w1 = lambda w: w.reshape(CMP_LEN * HEAD_DIM, -1).astype(BF16)
    const = lambda shape: pl.BlockSpec(shape, lambda b, g: (0,) * len(shape))
    return pl.pallas_call(
        _compress_kernel,
        grid=(B, B_KV),
        in_specs=[
            pl.BlockSpec((1, 1, nch, cw), lambda b, g: (b, g, 0, 0)),
            pl.BlockSpec((1, 1, nch, cw), lambda b, g: (b, B_KV + g, 0, 0)),
            const((2, cw)), const((2, cw)),
            const((CMP_LEN * HEAD_DIM, HEAD_DIM)), const((CMP_LEN * HEAD_DIM, HEAD_DIM)),
            const((HEAD_DIM, HEAD_DIM)), const((HEAD_DIM, HEAD_DIM)),
        ],
        out_specs=pl.BlockSpec((1, 2, 1, nch, HEAD_DIM), lambda b, g: (b, 0, g, 0, 0)),
        out_shape=jax.ShapeDtypeStruct((B, 2, B_KV, nch, HEAD_DIM), BF16),
        compiler_params=_cparams("parallel", "parallel"),
    )(chunks, chunks, pe(pe_k), pe(pe_v), w1(w1_k), w1(w1_v), w2_k.astype(BF16), w2_v.astype(BF16))


def _flash_step(s, v, state, mask=None):
    m, l, acc = state
    if mask is not None:
        s = jnp.where(mask, s, NEG)
    m_new = jnp.maximum(m, jnp.max(s, axis=1, keepdims=True))
    p = jnp.exp(s - m_new)
    if mask is not None:
        p = jnp.where(mask, p, 0.0)
    alpha = jnp.exp(m - m_new)
    l = alpha * l + jnp.sum(p, axis=1, keepdims=True)
    acc = alpha * acc + jnp.dot(p.astype(BF16), v, preferred_element_type=F32)
    return m_new, l, acc


def _attn_b_kernel(q_ref, ksel_ref, vsel_ref, kwin_ref, vwin_ref, kc_ref, vc_ref, z_ref, gt_ref, gb_ref,
                   cov_ref, oh_ref, y_ref, kaug_ref, *, S):
    TQ = B_TQ
    R = B_REP * TQ
    n_slc = S // SEL_LEN
    ncmp = S // CMP_STRIDE
    qi = pl.program_id(2)
    t0 = qi * TQ

    @pl.when(qi == 0)
    def _():
        kaug_ref[:, 0:HEAD_DIM] = ksel_ref[0, 0, 0]
        kaug_ref[:, HEAD_DIM:2 * HEAD_DIM] = oh_ref[...]

    q = q_ref[0, 0].reshape(R, HEAD_DIM)
    kc = kc_ref[0, 0, 0]
    vc = vc_ref[0, 0, 0]
    dn_t = (((1,), (1,)), ((), ()))

    s = lax.dot_general(q, kc, dn_t, preferred_element_type=F32) * SCALE
    tq = t0 + lax.broadcasted_iota(jnp.int32, (R, ncmp), 0) % TQ
    blk_end = lax.broadcasted_iota(jnp.int32, (R, ncmp), 1) * CMP_STRIDE + (CMP_LEN - 1)
    cmask = blk_end <= tq
    sm = jnp.where(cmask, s, NEG)
    m = jnp.max(sm, axis=1, keepdims=True)
    e = jnp.where(cmask, jnp.exp(sm - m), 0.0)
    den = jnp.sum(e, axis=1, keepdims=True)
    o_cmp = jnp.dot(e.astype(BF16), vc, preferred_element_type=F32) / jnp.maximum(den, 1e-30)

    st = lax.dot_general(kc, q, dn_t, preferred_element_type=F32) * SCALE
    tq_t = t0 + lax.broadcasted_iota(jnp.int32, (ncmp, R), 1) % TQ
    blk_end_t = lax.broadcasted_iota(jnp.int32, (ncmp, R), 0) * CMP_STRIDE + (CMP_LEN - 1)
    cmask_t = blk_end_t <= tq_t
    smt = jnp.where(cmask_t, st, NEG)
    mt = jnp.max(smt, axis=0, keepdims=True)
    et = jnp.where(cmask_t, jnp.exp(smt - mt), 0.0)
    pt = et / jnp.maximum(jnp.sum(et, axis=0, keepdims=True), 1e-30)
    psum = pt[:, 0:TQ]
    for r in range(1, B_REP):
        psum = psum + pt[:, r * TQ:(r + 1) * TQ]
    p_hi = psum.astype(BF16)
    p_lo = (psum - p_hi.astype(F32)).astype(BF16)
    cov = cov_ref[...]
    p_slc = (jnp.dot(cov, p_hi, preferred_element_type=F32)
             + jnp.dot(cov, p_lo, preferred_element_type=F32))
    nblk = lax.broadcasted_iota(jnp.int32, (n_slc, TQ), 0)
    cur = (t0 + lax.broadcasted_iota(jnp.int32, (n_slc, TQ), 1)) // SEL_LEN
    forced = (nblk == 0) | (nblk == cur) | (nblk == cur - 1)
    allowed = nblk <= cur
    score = jnp.where(allowed, jnp.where(forced, FORCE_SCORE, p_slc), -jnp.inf)
    rank = jnp.zeros((n_slc, TQ), F32)
    for mblk in range(n_slc):
        row = score[mblk:mblk + 1, :]
        beats = (row > score) | ((row == score) & (nblk > mblk))
        rank = rank + jnp.where(beats, 1.0, 0.0)
    sel_t = jnp.where((rank < N_SELECT) & allowed, 1.0, 0.0)
    sel_pad = jnp.concatenate([sel_t, jnp.zeros((LANES - n_slc, TQ), F32)], axis=0)
    sel = sel_pad.T
    lane = lax.broadcasted_iota(jnp.int32, (TQ, LANES), 1)
    bias = jnp.where((lane < n_slc) & (sel < 0.5), NEG, 0.0).astype(BF16)
    qaug = jnp.concatenate([q, jnp.concatenate([bias] * B_REP, axis=0)], axis=1)

    row_t = lax.broadcasted_iota(jnp.int32, (R, TQ), 0) % TQ
    col_t = lax.broadcasted_iota(jnp.int32, (R, TQ), 1)
    init = (jnp.full((R, 1), NEG, F32), jnp.zeros((R, 1), F32), jnp.zeros((R, HEAD_DIM), F32))

    def sel_step(c, state):
        rows = pl.ds(pl.multiple_of(c * TQ, TQ), TQ)
        s = lax.dot_general(qaug, kaug_ref[rows, :], dn_t, preferred_element_type=F32) * SCALE
        return _flash_step(s, vsel_ref[0, 0, 0, rows, :], state)

    state = lax.fori_loop(0, qi, sel_step, init)
    rows_d = pl.ds(pl.multiple_of(t0, TQ), TQ)
    s = lax.dot_general(qaug, kaug_ref[rows_d, :], dn_t, preferred_element_type=F32) * SCALE
    m, l, acc = _flash_step(s, vsel_ref[0, 0, 0, rows_d, :], state, mask=col_t <= row_t)
    o_sel = acc / l

    nprev = -(-(WIN - 1) // TQ)

    def win_step(c, state):
        rows = pl.ds(pl.multiple_of(c * TQ, TQ), TQ)
        s = lax.dot_general(q, kwin_ref[0, 0, 0, rows, :], dn_t, preferred_element_type=F32) * SCALE
        dist = (t0 - c * TQ) + row_t - col_t
        return _flash_step(s, vwin_ref[0, 0, 0, rows, :], state, mask=dist <= WIN - 1)

    state = lax.fori_loop(jnp.maximum(qi - nprev, 0), qi, win_step, init)
    s = lax.dot_general(q, kwin_ref[0, 0, 0, rows_d, :], dn_t, preferred_element_type=F32) * SCALE
    m, l, acc = _flash_step(s, vwin_ref[0, 0, 0, rows_d, :], state, mask=col_t <= row_t)
    o_win = acc / l

    gates = _sigmoid(gt_ref[0, 0, 0] + gb_ref[0])
    for r in range(B_REP):
        rs = slice(r * TQ, (r + 1) * TQ)
        o = (gates[:, 3 * r:3 * r + 1] * o_cmp[rs]
             + gates[:, 3 * r + 1:3 * r + 2] * o_sel[rs]
             + gates[:, 3 * r + 2:3 * r + 3] * o_win[rs])
        y_ref[0, r] = (o * _silu(z_ref[0, 0, r])).astype(BF16)


def _attn_b(hb, hf, kvc, gate_b):
    _, B, _, S, _ = hb.shape
    TQ = B_TQ
    n_slc = S // SEL_LEN
    ncmp = S // CMP_STRIDE
    j = np.arange(ncmp)[None, :]
    n = np.arange(n_slc)[:, None]
    cover_t = ((j * CMP_STRIDE < (n + 1) * SEL_LEN) & (j * CMP_STRIDE + CMP_LEN > n * SEL_LEN)
               & (j < ncmp - 1)).astype(np.float32)
    onehot = (np.arange(S)[:, None] // SEL_LEN == np.arange(LANES)[None, :]).astype(np.float32)
    nq = B_REP * B_KV
    kv = lambda s0: pl.BlockSpec((1, 1, 1, S, LANES), lambda b, g, i, s0=s0: (0, b, s0 + g, 0, 0))
    return pl.pallas_call(
        functools.partial(_attn_b_kernel, S=S),
        grid=(B, B_KV, S // TQ),
        in_specs=[
            pl.BlockSpec((1, 1, B_REP, TQ, LANES), lambda b, g, i: (0, b, g, i, 0)),
            kv(nq), kv(nq + 2 * B_KV), kv(nq + B_KV), kv(nq + 3 * B_KV),
            pl.BlockSpec((1, 1, 1, ncmp, LANES), lambda b, g, i: (b, 0, g, 0, 0)),
            pl.BlockSpec((1, 1, 1, ncmp, LANES), lambda b, g, i: (b, 1, g, 0, 0)),
            pl.BlockSpec((1, 1, B_REP, TQ, LANES), lambda b, g, i: (0, b, 1 + g, i, 0)),
            pl.BlockSpec((1, 1, 1, TQ, LANES), lambda b, g, i: (0, b, 12 + g, i, 0)),
            pl.BlockSpec((1, 1, LANES), lambda b, g, i: (g, 0, 0)),
            pl.BlockSpec((n_slc, ncmp), lambda b, g, i: (0, 0)),
            pl.BlockSpec((S, LANES), lambda b, g, i: (0, 0)),
        ],
        out_specs=pl.BlockSpec((1, B_REP, TQ, LANES), lambda b, g, i: (b, g, i, 0)),
        out_shape=jax.ShapeDtypeStruct((B, nq, S, LANES), BF16),
        scratch_shapes=[pltpu.VMEM((S, 2 * HEAD_DIM), BF16)],
        compiler_params=_cparams("parallel", "parallel", "arbitrary"),
    )(hb, hb, hb, hb, hb, kvc, kvc, hf, hf, gate_b, jnp.asarray(cover_t, BF16), jnp.asarray(onehot, BF16))


def _mixer_b(x, gain, w_in, gate_b, pe_k, w1_k, w2_k, pe_v, w1_v, w2_v, w_out, tables, final_gain=None):
    B, S, D = x.shape
    bw = D
    kvw = B_KV * HEAD_DIM
    o0, o1, o2 = bw, bw + 6 * kvw, 2 * bw + 6 * kvw
    W = w_in.astype(BF16)
    kvcol = lambda c: W[:, o0 + c * kvw:o0 + (c + 1) * kvw]
    wg = W[:, o2:].reshape(D, B_KV, B_REP * 3)
    wg = jnp.pad(wg, ((0, 0), (0, 0), (0, LANES - B_REP * 3))).reshape(D, B_KV * LANES)
    gb = jnp.pad(gate_b.reshape(B_KV, 1, B_REP * 3), ((0, 0), (0, 0), (0, LANES - B_REP * 3)))
    w_b = jnp.concatenate([W[:, :o0], kvcol(2), kvcol(4), kvcol(3), kvcol(5)], axis=1)
    hb = _norm_proj(x, gain, w_b, d=1, n_rope=(bw + 2 * kvw) // LANES, out_dtype=BF16, tables=tables)
    w_f = jnp.concatenate([kvcol(0), kvcol(1), W[:, o1:o2], wg], axis=1)
    hf = _norm_proj(x, gain, w_f, d=1, n_rope=kvw // LANES, out_dtype=F32, tables=tables, tn=256)
    kvc = _compress(hf[0], pe_k, w1_k, w2_k, pe_v, w1_v, w2_v)
    y = _attn_b(hb, hf, kvc, gb)
    return _out_proj(y, w_out.astype(BF16), x, final_gain)


def _rglru_kernel(xb_ref, z_ref, cw_ref, cb_ref, wa_ref, ba_ref, wx_ref, bx_ref, lam_ref, y_ref, xpad_ref, *, S):
    PAD = 8
    xpad_ref[0:PAD, :] = jnp.zeros((PAD, LANES), F32)
    xpad_ref[PAD:PAD + S, :] = xb_ref[0, 0, 0]
    nlam = -lam_ref[...]
    softplus = jnp.maximum(nlam, 0.0) + jnp.log(1.0 + jnp.exp(-jnp.abs(nlam)))
    row = lax.broadcasted_iota(jnp.int32, (C_CHUNK, LANES), 0)

    def chunk(c, h_prev):
        base = c * C_CHUNK
        xc = jnp.broadcast_to(cb_ref[...], (C_CHUNK, LANES))
        for k in range(CONV_W):
            xc = xc + cw_ref[k:k + 1, :] * xpad_ref[pl.ds(base + PAD - (CONV_W - 1) + k, C_CHUNK), :]
        xcb = xc.astype(BF16)
        r = _sigmoid(jnp.dot(xcb, wa_ref[0], preferred_element_type=F32) + ba_ref[...])
        i = _sigmoid(jnp.dot(xcb, wx_ref[0], preferred_element_type=F32) + bx_ref[...])
        log_a = -LRU_C * r * softplus
        a = jnp.exp(log_a)
        b = jnp.sqrt(1.0 - a * a) * i * xc
        k = 1
        while k < C_CHUNK:
            keep = row >= k
            a_s = jnp.where(keep, pltpu.roll(a, k, 0), 1.0)
            b_s = jnp.where(keep, pltpu.roll(b, k, 0), 0.0)
            b = a * b_s + b
            a = a * a_s
            k *= 2
        h = b + a * h_prev
        rows = pl.ds(pl.multiple_of(base, C_CHUNK), C_CHUNK)
        y_ref[0, 0, rows, :] = (h * _silu(z_ref[0, 0, 0, rows, :])).astype(BF16)
        return h[C_CHUNK - 1:C_CHUNK, :]

    lax.fori_loop(0, S // C_CHUNK, chunk, jnp.zeros((1, LANES), F32))


def _rglru(h, conv_w, conv_b, wa, ba, wx, bx, lam):
    _, B, _, S, _ = h.shape
    nbk = C_BLOCKS
    vec = lambda v: v.reshape(1, nbk * LANES)
    vspec = pl.BlockSpec((1, LANES), lambda b, n: (0, n))
    mspec = pl.BlockSpec((1, LANES, LANES), lambda b, n: (n, 0, 0))
    return pl.pallas_call(
        functools.partial(_rglru_kernel, S=S),
        grid=(B, nbk),
        in_specs=[
            pl.BlockSpec((1, 1, 1, S, LANES), lambda b, n: (0, b, n, 0, 0)),
            pl.BlockSpec((1, 1, 1, S, LANES), lambda b, n: (0, b, nbk + n, 0, 0)),
            pl.BlockSpec((CONV_W, LANES), lambda b, n: (0, n)),
            vspec, mspec, vspec, mspec, vspec, vspec,
        ],
        out_specs=pl.BlockSpec((1, 1, S, LANES), lambda b, n: (b, n, 0, 0)),
        out_shape=jax.ShapeDtypeStruct((B, nbk, S, LANES), BF16),
        scratch_shapes=[pltpu.VMEM((S + 8, LANES), F32)],
        compiler_params=_cparams("parallel", "parallel"),
    )(h, h, conv_w, vec(conv_b), wa.astype(BF16), vec(ba), wx.astype(BF16), vec(bx), vec(lam))


def _mixer_c(x, gain, w_in, conv_w, conv_b, wa, ba, wx, bx, lam, w_out, final_gain=None):
    h = _norm_proj(x, gain, w_in.astype(BF16), d=1, n_rope=0, out_dtype=F32)
    y = _rglru(h, conv_w, conv_b, wa, ba, wx, bx, lam)
    return _out_proj(y, w_out.astype(BF16), x, final_gain)


def kernel(x, norm_g, final_g, a_w_in, a_w_out, b_w_in, b_gate_b, b_pe_k, b_w1_k, b_w2_k, b_pe_v, b_w1_v,
           b_w2_v, b_w_out, c_w_in, c_conv_w, c_conv_b, c_wa, c_ba, c_wx, c_bx, c_lambda, c_w_out):
    depth = norm_g.shape[0]
    tables = _rope_tables(x.shape[1])
    for i in range(depth):
        kind, j = i % 3, i // 3
        final = final_g if i == depth - 1 else None
        if kind == 0:
            x = _mixer_a(x, norm_g[i], a_w_in[j], a_w_out[j], tables, final)
        elif kind == 1:
            x = _mixer_b(x, norm_g[i], b_w_in[j], b_gate_b[j], b_pe_k[j], b_w1_k[j], b_w2_k[j],
                         b_pe_v[j], b_w1_v[j], b_w2_v[j], b_w_out[j], tables, final)
        else:
            x = _mixer_c(x, norm_g[i], c_w_in[j], c_conv_w[j], c_conv_b[j], c_wa[j], c_ba[j],
                         c_wx[j], c_bx[j], c_lambda[j], c_w_out[j], final)
    return x
```

```python
import functools

import numpy as np
import jax
import jax.numpy as jnp
from jax import lax
from jax.experimental import pallas as pl
from jax.experimental.pallas import tpu as pltpu

F32 = jnp.float32
BF16 = jnp.bfloat16

HEAD_DIM = 128
LANES = 128
ROPE_DIM = HEAD_DIM // 4
ROPE_HALF = ROPE_DIM // 2
ROPE_THETA = 500000.0
NORM_EPS = 1e-6
SCALE = HEAD_DIM ** -0.5
NEG = -1e30
VMEM_LIMIT = 48 * 1024 * 1024

A_GROUPS = ((128, 1), (512, 4), (2048, 16))
A_BLK = 128
A_UNROLL = 8
B_KV = 2
B_REP = 4
CMP_LEN = 32
CMP_STRIDE = 16
SEL_LEN = 64
N_SELECT = 16
WIN = 512
FORCE_SCORE = 1000.0
B_TQ = 256
C_BLOCKS = 10
CONV_W = 4
LRU_C = 8.0
C_CHUNK = 256


def _cparams(*sem):
    return pltpu.CompilerParams(dimension_semantics=sem, vmem_limit_bytes=VMEM_LIMIT)


def _sigmoid(x):
    return 1.0 / (1.0 + jnp.exp(-x))


def _silu(x):
    return x * _sigmoid(x)


ROPE_SWAP = LANES // 2


def _rope_layout(w):
    lead = w.shape[:-1]
    h = w.reshape(*lead, -1, HEAD_DIM)
    h = jnp.concatenate([h[..., :ROPE_HALF], h[..., ROPE_DIM:ROPE_SWAP + ROPE_HALF],
                         h[..., ROPE_HALF:ROPE_DIM], h[..., ROPE_SWAP + ROPE_HALF:]], axis=-1)
    return h.reshape(*lead, -1)


def _rope_tables(S):
    inv_freq = ROPE_THETA ** (-2.0 * jnp.arange(ROPE_HALF, dtype=F32) / ROPE_DIM)
    ang = jnp.arange(S, dtype=F32)[:, None] * inv_freq[None, :]
    cos, sin = jnp.cos(ang), jnp.sin(ang)
    gap = ROPE_SWAP - ROPE_HALF
    c = jnp.concatenate([cos, jnp.ones((S, gap), F32), cos, jnp.ones((S, gap), F32)], axis=1)
    sn = jnp.concatenate([-sin, jnp.zeros((S, gap), F32), sin, jnp.zeros((S, gap), F32)], axis=1)
    return c, sn


PROJ_TN = 512


def _norm_proj_kernel(*refs, nb, lt, parts, any_rope):
    n_out = len(parts)
    o_refs, xn_ref = refs[-n_out - 1:-1], refs[-1]
    if any_rope:
        x_ref, g_ref, w_ref, c_ref, sn_ref = refs[:5]
    else:
        x_ref, g_ref, w_ref = refs[:3]
    xf = x_ref[...]
    ms = jnp.mean(xf * xf, axis=-1, keepdims=True)
    xn_ref[...] = (xf * lax.rsqrt(ms + NORM_EPS) * g_ref[...]).astype(BF16)
    col = 0
    for o_ref, (ns, _, n_rope) in zip(o_refs, parts):
        hs = PROJ_TN // LANES if ns % (PROJ_TN // LANES) == 0 else 2
        assert ns % hs == 0
        for c in range(ns // hs):
            tn = hs * LANES
            res = jnp.dot(xn_ref[...], w_ref[:, col:col + tn], preferred_element_type=F32)
            col += tn
            for bb in range(nb):
                for hh in range(hs):
                    sub = res[bb * lt:(bb + 1) * lt, hh * LANES:(hh + 1) * LANES]
                    if c * hs + hh < n_rope:
                        sub = sub * c_ref[...] + pltpu.roll(sub, ROPE_SWAP, 1) * sn_ref[...]
                    o_ref[0, bb, c * hs + hh] = sub.astype(o_ref.dtype)


def _norm_proj(x, gain, w, parts, *, d, tables, tm=512):
    B, S, D = x.shape
    N = w.shape[1]
    L = S // d
    rows = B * L
    tm = min(tm, rows)
    assert N == LANES * sum(p[0] for p in parts) and rows % tm == 0
    assert (tm % L == 0) or (L % tm == 0)
    if tm >= L:
        nb, lt = tm // L, L
        o_spec = lambda ns: pl.BlockSpec((1, nb, ns, L, LANES), lambda r, mi: (r, mi, 0, 0, 0))
        t_map = lambda r, mi: (0, r)
    else:
        nb, lt, tps = 1, tm, L // tm
        o_spec = lambda ns: pl.BlockSpec((1, 1, ns, tm, LANES), lambda r, mi: (r, mi // tps, 0, mi % tps, 0))
        t_map = lambda r, mi: (mi % tps, r)
    in_specs = [
        pl.BlockSpec((tm, D), lambda r, mi: (mi, r)),
        pl.BlockSpec((1, D), lambda r, mi: (0, 0)),
        pl.BlockSpec((D, N), lambda r, mi: (0, 0)),
    ]
    args = [x.reshape(rows, d * D), gain.reshape(1, D), w]
    any_rope = any(p[2] for p in parts)
    if any_rope:
        in_specs += [pl.BlockSpec((lt, LANES), t_map)] * 2
        args += [t.reshape(L, d * LANES) for t in tables]
    return pl.pallas_call(
        functools.partial(_norm_proj_kernel, nb=nb, lt=lt, parts=tuple(parts), any_rope=any_rope),
        grid=(d, rows // tm),
        in_specs=in_specs,
        out_specs=[o_spec(p[0]) for p in parts],
        out_shape=[jax.ShapeDtypeStruct((d, B, p[0], L, LANES), p[1]) for p in parts],
        scratch_shapes=[pltpu.VMEM((tm, D), BF16)],
        compiler_params=_cparams("parallel", "parallel"),
        name=f"norm_proj_d{d}_n{N}",
    )(*args)


def _out_proj_kernel(*refs, nh, final):
    if final:
        y_ref, w_ref, x_ref, g_ref, o_ref = refs
    else:
        y_ref, w_ref, x_ref, o_ref = refs
    y = jnp.concatenate([y_ref[0, h] for h in range(nh)], axis=1)
    xn = x_ref[0] + jnp.dot(y, w_ref[...], preferred_element_type=F32)
    if final:
        ms = jnp.mean(xn * xn, axis=-1, keepdims=True)
        xn = xn * lax.rsqrt(ms + NORM_EPS) * g_ref[...]
    o_ref[0] = xn


def _out_proj(y, w, x, final_gain=None, tm=512):
    B, nh, S, _ = y.shape
    D = x.shape[-1]
    final = final_gain is not None
    in_specs = [
        pl.BlockSpec((1, nh, tm, LANES), lambda b, i: (b, 0, i, 0)),
        pl.BlockSpec((nh * LANES, D), lambda b, i: (0, 0)),
        pl.BlockSpec((1, tm, D), lambda b, i: (b, i, 0)),
    ]
    args = [y, w, x]
    if final:
        in_specs.append(pl.BlockSpec((1, D), lambda b, i: (0, 0)))
        args.append(final_gain.reshape(1, D))
    return pl.pallas_call(
        functools.partial(_out_proj_kernel, nh=nh, final=final),
        grid=(B, S // tm),
        in_specs=in_specs,
        out_specs=pl.BlockSpec((1, tm, D), lambda b, i: (b, i, 0)),
        out_shape=jax.ShapeDtypeStruct((B, S, D), F32),
        compiler_params=_cparams("parallel", "parallel"),
        name="out_proj",
    )(*args)


def _attn_a_kernel(q1, k1, v1, q2, k2, v2, q3, k3, v3, z_ref, y_ref, o_s, l_s, t_s, b_s, *, S):
    groups = ((q1, k1, v1), (q2, k2, v2), (q3, k3, v3))
    nblk = S // A_BLK
    PS = 4

    a = lax.broadcasted_iota(jnp.int32, (A_BLK, 2 * A_BLK), 0)
    c = lax.broadcasted_iota(jnp.int32, (A_BLK, 2 * A_BLK), 1)
    b_s[0] = jnp.where(c <= a, 0.0, NEG)
    b_s[1] = jnp.where((c >= a) & (c <= a + A_BLK), 0.0, NEG)

    for gi, ((q_ref, k_ref, v_ref), (win, d)) in enumerate(zip(groups, A_GROUPS)):
        L = S // d
        nqb = L // A_BLK
        nk = 2 * A_BLK if nqb > 1 else A_BLK
        assert win // d == A_BLK and d in (1, PS, PS * PS)

        def blk(tt, carry, q_ref=q_ref, k_ref=k_ref, v_ref=v_ref, d=d, nqb=nqb, nk=nk, gi=gi):
            idx, ss, vs = [], [], []
            for u in range(A_UNROLL):
                t = tt * A_UNROLL + u
                r = t // nqb
                i = t % nqb
                ks = jnp.maximum(i - 1, 0) * A_BLK
                q = q_ref[r, 0, 0, pl.ds(pl.multiple_of(i * A_BLK, A_BLK), A_BLK), :]
                k = k_ref[r, 0, 0, pl.ds(pl.multiple_of(ks, A_BLK), nk), :]
                vs.append(v_ref[r, 0, 0, pl.ds(pl.multiple_of(ks, A_BLK), nk), :])
                ss.append(lax.dot_general(q, k, (((1,), (1,)), ((), ())), preferred_element_type=F32))
                idx.append((t, r, i))
            ps, ms, ls = [], [], []
            for u in range(A_UNROLL):
                s = ss[u] * SCALE + b_s[jnp.minimum(idx[u][2], 1), :, 0:nk]
                m = jnp.max(s, axis=1, keepdims=True)
                p = jnp.exp(s - m)
                ls.append(jnp.sum(p, axis=1, keepdims=True))
                ms.append(m)
                ps.append(p.astype(BF16))
            for u in range(A_UNROLL):
                t, r, i = idx[u]
                o = jnp.dot(ps[u], vs[u], preferred_element_type=F32) / ls[u]
                lse = jnp.broadcast_to(ms[u] + jnp.log(ls[u]), (A_BLK, LANES))
                if d == 1:
                    rows = pl.ds(pl.multiple_of(t * A_BLK, A_BLK), A_BLK)
                    o_s[gi, rows, :] = o
                    l_s[gi, rows, :] = lse
                elif d == PS:
                    rows = pl.ds(i * (A_BLK * d) + r, A_BLK, stride=d)
                    o_s[gi, rows, :] = o
                    l_s[gi, rows, :] = lse
                else:
                    rows = pl.ds(r // PS, A_BLK, stride=PS)
                    t_s[0, r % PS, rows, :] = o
                    t_s[1, r % PS, rows, :] = lse
            return carry

        lax.fori_loop(0, nblk // A_UNROLL, blk, 0)
        if d == PS * PS:
            for r0 in range(PS):
                def second(cidx, carry, r0=r0, gi=gi):
                    src = pl.ds(pl.multiple_of(cidx * A_BLK, A_BLK), A_BLK)
                    dst = pl.ds(cidx * (A_BLK * PS) + r0, A_BLK, stride=PS)
                    o_s[gi, dst, :] = t_s[0, r0, src, :]
                    l_s[gi, dst, :] = t_s[1, r0, src, :]
                    return carry
                lax.fori_loop(0, S // PS // A_BLK, second, 0)

    def merge(c, carry):
        rows = pl.ds(pl.multiple_of(c * A_BLK, A_BLK), A_BLK)
        l1, l2, l3 = l_s[0, rows, :], l_s[1, rows, :], l_s[2, rows, :]
        mx = jnp.maximum(jnp.maximum(l1, l2), l3)
        e1, e2, e3 = jnp.exp(l1 - mx), jnp.exp(l2 - mx), jnp.exp(l3 - mx)
        o = (e1 * o_s[0, rows, :] + e2 * o_s[1, rows, :] + e3 * o_s[2, rows, :]) / (e1 + e2 + e3)
        y_ref[0, 0, rows, :] = (o * _silu(z_ref[0, 0, 0, rows, :])).astype(BF16)
        return carry

    lax.fori_loop(0, nblk, merge, 0)


def _attn_a(qkv, z):
    _, B, nh, S, _ = z.shape
    in_specs, args = [], []
    for arr, (_, d) in zip(qkv, A_GROUPS):
        L = S // d
        for t in range(3):
            in_specs.append(pl.BlockSpec((d, 1, 1, L, LANES), lambda b, h, t=t, nh=nh: (0, b, t * nh + h, 0, 0)))
            args.append(arr)
    in_specs.append(pl.BlockSpec((1, 1, 1, S, LANES), lambda b, h: (0, b, h, 0, 0)))
    args.append(z)
    return pl.pallas_call(
        functools.partial(_attn_a_kernel, S=S),
        grid=(B, nh),
        in_specs=in_specs,
        out_specs=pl.BlockSpec((1, 1, S, LANES), lambda b, h: (b, h, 0, 0)),
        out_shape=jax.ShapeDtypeStruct((B, nh, S, LANES), BF16),
        scratch_shapes=[pltpu.VMEM((3, S, LANES), F32), pltpu.VMEM((3, S, LANES), F32),
                        pltpu.VMEM((2, 4, S // 4, LANES), F32), pltpu.VMEM((2, A_BLK, 2 * A_BLK), F32)],
        compiler_params=_cparams("parallel", "parallel"),
        name="attn_a",
    )(*args)


def _mixer_a(x, gain, w_in, w_out, tables, final_gain=None):
    B, S, D = x.shape
    W = w_in.astype(BF16)
    n_g = len(A_GROUPS)
    aw = D
    nh = aw // LANES
    qkv = []
    for g, (_, d) in enumerate(A_GROUPS):
        c0 = g * 3 * aw
        wg = [_rope_layout(W[:, c0:c0 + 2 * aw]), W[:, c0 + 2 * aw:c0 + 3 * aw]]
        parts = [(3 * nh, BF16, 2 * nh)]
        if d == 1:
            wg.append(W[:, n_g * 3 * aw:])
            parts.append((nh, F32, 0))
        outs = _norm_proj(x, gain, jnp.concatenate(wg, axis=1), parts, d=d, tables=tables)
        qkv.append(outs[0])
        if d == 1:
            z = outs[1]
    y = _attn_a(qkv, z)
    return _out_proj(y, w_out.astype(BF16), x, final_gain)


def _compress_kernel(k_ref, v_ref, pek_ref, pev_ref, w1k_ref, w1v_ref, w2k_ref, w2v_ref, o_ref):
    def one(c_ref, pe_ref, w1_ref, w2_ref):
        half = c_ref.shape[-1]
        ch = c_ref[0, 0]
        u = jnp.dot((ch + pe_ref[0:1, :]).astype(BF16), w1_ref[0:half, :], preferred_element_type=F32)
        v = jnp.dot((ch + pe_ref[1:2, :]).astype(BF16), w1_ref[half:2 * half, :], preferred_element_type=F32)
        h = _silu(u + pltpu.roll(v, v.shape[0] - 1, 0))
        return jnp.dot(h.astype(BF16), w2_ref[...], preferred_element_type=F32).astype(BF16)

    o_ref[0, 0, 0] = one(k_ref, pek_ref, w1k_ref, w2k_ref)
    o_ref[0, 1, 0] = one(v_ref, pev_ref, w1v_ref, w2v_ref)


def _compress(cmp_arr, pe_k, w1_k, w2_k, pe_v, w1_v, w2_v):
    B, ns, S, _ = cmp_arr.shape
    nch = S // CMP_STRIDE
    cw = CMP_STRIDE * HEAD_DIM
    chunks = cmp_arr.reshape(B, ns, nch, cw)
    pe = lambda p: p.reshape(2, cw)
    w1 = lambda w: w.reshape(CMP_LEN * HEAD_DIM, -1).astype(BF16)
    const = lambda shape: pl.BlockSpec(shape, lambda b, g: (0,) * len(shape))
    return pl.pallas_call(
        _compress_kernel,
        grid=(B, B_KV),
        in_specs=[
            pl.BlockSpec((1, 1, nch, cw), lambda b, g: (b, g, 0, 0)),
            pl.BlockSpec((1, 1, nch, cw), lambda b, g: (b, B_KV + g, 0, 0)),
            const((2, cw)), const((2, cw)),
            const((CMP_LEN * HEAD_DIM, HEAD_DIM)), const((CMP_LEN * HEAD_DIM, HEAD_DIM)),
            const((HEAD_DIM, HEAD_DIM)), const((HEAD_DIM, HEAD_DIM)),
        ],
        out_specs=pl.BlockSpec((1, 2, 1, nch, HEAD_DIM), lambda b, g: (b, 0, g, 0, 0)),
        out_shape=jax.ShapeDtypeStruct((B, 2, B_KV, nch, HEAD_DIM), BF16),
        compiler_params=_cparams("parallel", "parallel"),
        name="compress",
    )(chunks, chunks, pe(pe_k), pe(pe_v), w1(w1_k), w1(w1_v), w2_k.astype(BF16), w2_v.astype(BF16))


def _halves(x):
    return x[:, 0:LANES], x[:, LANES:2 * LANES]


def _attn_b_kernel(q_ref, ksel_ref, vsel_ref, kwin_ref, vwin_ref, kc_ref, vc_ref, z_ref, gt_ref, gb_ref,
                   cov_ref, oh_ref, y_ref, kaug_ref, s_ref, mp_ref, lp_ref, acc_ref, *, S):
    TQ = B_TQ
    R = B_REP * TQ
    n_slc = S // SEL_LEN
    ncmp = S // CMP_STRIDE
    qi = pl.program_id(2)
    t0 = qi * TQ

    @pl.when(qi == 0)
    def _():
        kaug_ref[:, 0:HEAD_DIM] = ksel_ref[0, 0, 0]
        kaug_ref[:, HEAD_DIM:2 * HEAD_DIM] = oh_ref[...]

    q = q_ref[0, 0].reshape(R, HEAD_DIM)
    kc = kc_ref[0, 0, 0]
    vc = vc_ref[0, 0, 0]
    dn_t = (((1,), (1,)), ((), ()))

    s = lax.dot_general(q, kc, dn_t, preferred_element_type=F32) * SCALE
    tq = t0 + lax.broadcasted_iota(jnp.int32, (R, ncmp), 0) % TQ
    blk_end = lax.broadcasted_iota(jnp.int32, (R, ncmp), 1) * CMP_STRIDE + (CMP_LEN - 1)
    cmask = blk_end <= tq
    sm = jnp.where(cmask, s, NEG)
    m = jnp.max(sm, axis=1, keepdims=True)
    e = jnp.where(cmask, jnp.exp(sm - m), 0.0)
    den = jnp.sum(e, axis=1, keepdims=True)
    o_cmp = jnp.dot(e.astype(BF16), vc, preferred_element_type=F32) / jnp.maximum(den, 1e-30)

    st = lax.dot_general(kc, q, dn_t, preferred_element_type=F32) * SCALE
    tq_t = t0 + lax.broadcasted_iota(jnp.int32, (ncmp, R), 1) % TQ
    blk_end_t = lax.broadcasted_iota(jnp.int32, (ncmp, R), 0) * CMP_STRIDE + (CMP_LEN - 1)
    cmask_t = blk_end_t <= tq_t
    smt = jnp.where(cmask_t, st, NEG)
    mt = jnp.max(smt, axis=0, keepdims=True)
    et = jnp.where(cmask_t, jnp.exp(smt - mt), 0.0)
    pt = et / jnp.maximum(jnp.sum(et, axis=0, keepdims=True), 1e-30)
    psum = pt[:, 0:TQ]
    for r in range(1, B_REP):
        psum = psum + pt[:, r * TQ:(r + 1) * TQ]
    p_hi = psum.astype(BF16)
    p_lo = (psum - p_hi.astype(F32)).astype(BF16)
    cov = cov_ref[...]
    p_slc = (jnp.dot(cov, p_hi, preferred_element_type=F32)
             + jnp.dot(cov, p_lo, preferred_element_type=F32))
    nblk = lax.broadcasted_iota(jnp.int32, (n_slc, TQ), 0)
    cur = (t0 + lax.broadcasted_iota(jnp.int32, (n_slc, TQ), 1)) // SEL_LEN
    forced = (nblk == 0) | (nblk == cur) | (nblk == cur - 1)
    allowed = nblk <= cur
    score = jnp.where(allowed, jnp.where(forced, FORCE_SCORE, p_slc), -jnp.inf)
    rank = jnp.zeros((n_slc, TQ), F32)
    for mblk in range(n_slc):
        row = score[mblk:mblk + 1, :]
        beats = (row > score) | ((row == score) & (nblk > mblk))
        rank = rank + jnp.where(beats, 1.0, 0.0)
    sel_t = jnp.where((rank < N_SELECT) & allowed, 1.0, 0.0)
    sel_pad = jnp.concatenate([sel_t, jnp.zeros((LANES - n_slc, TQ), F32)], axis=0)
    sel = sel_pad.T
    lane = lax.broadcasted_iota(jnp.int32, (TQ, LANES), 1)
    bias = jnp.where((lane < n_slc) & (sel < 0.5), NEG, 0.0).astype(BF16)
    qaug = jnp.concatenate([q, jnp.concatenate([bias] * B_REP, axis=0)], axis=1)

    row_t = lax.broadcasted_iota(jnp.int32, (R, TQ), 0) % TQ
    col_t = lax.broadcasted_iota(jnp.int32, (R, TQ), 1)
    rows_d = pl.ds(pl.multiple_of(t0, TQ), TQ)

    def sel_scores(c, carry):
        rows = pl.ds(pl.multiple_of(c * TQ, TQ), TQ)
        s = lax.dot_general(qaug, kaug_ref[rows, :], dn_t, preferred_element_type=F32)
        s_ref[c] = s
        s0, s1 = _halves(s)
        mp_ref[...] = jnp.maximum(mp_ref[...], jnp.maximum(s0, s1))
        return carry

    mp_ref[...] = jnp.full((R, LANES), NEG, F32)
    lax.fori_loop(0, qi, sel_scores, 0)
    s = lax.dot_general(qaug, kaug_ref[rows_d, :], dn_t, preferred_element_type=F32)
    s = jnp.where(col_t <= row_t, s, NEG)
    s_ref[qi] = s
    s0, s1 = _halves(s)
    m = jnp.max(jnp.maximum(mp_ref[...], jnp.maximum(s0, s1)), axis=1, keepdims=True)
    mp_ref[...] = jnp.broadcast_to(m, (R, LANES))
    lp_ref[...] = jnp.zeros((R, LANES), F32)
    acc_ref[...] = jnp.zeros((R, HEAD_DIM), F32)

    def sel_values(c, carry):
        rows = pl.ds(pl.multiple_of(c * TQ, TQ), TQ)
        s0, s1 = _halves(s_ref[c])
        mb = mp_ref[...]
        p0 = jnp.exp((s0 - mb) * SCALE)
        p1 = jnp.exp((s1 - mb) * SCALE)
        lp_ref[...] += p0 + p1
        p = jnp.concatenate([p0.astype(BF16), p1.astype(BF16)], axis=1)
        acc_ref[...] += jnp.dot(p, vsel_ref[0, 0, 0, rows, :], preferred_element_type=F32)
        return carry

    lax.fori_loop(0, qi + 1, sel_values, 0)
    o_sel = acc_ref[...] / jnp.sum(lp_ref[...], axis=1, keepdims=True)

    nprev = -(-(WIN - 1) // TQ)
    ss, vs = [], []
    for j in range(nprev + 1):
        cc = qi - nprev + j
        rows = pl.ds(pl.multiple_of(jnp.maximum(cc, 0) * TQ, TQ), TQ)
        s = lax.dot_general(q, kwin_ref[0, 0, 0, rows, :], dn_t, preferred_element_type=F32)
        dist = (nprev - j) * TQ + row_t - col_t
        if j == nprev:
            s = jnp.where(dist >= 0, s, NEG)
        else:
            if (nprev - j) * TQ + TQ - 1 > WIN - 1:
                s = jnp.where(dist <= WIN - 1, s, NEG)
            s = s + jnp.where(cc >= 0, 0.0, NEG)
        ss.append(s)
        vs.append(vwin_ref[0, 0, 0, rows, :])
    mw = ss[0]
    for s in ss[1:]:
        mw = jnp.maximum(mw, s)
    mw0, mw1 = _halves(mw)
    m = jnp.max(jnp.maximum(mw0, mw1), axis=1, keepdims=True)
    mb = jnp.broadcast_to(m, (R, LANES))
    lw = jnp.zeros((R, LANES), F32)
    o_win = jnp.zeros((R, HEAD_DIM), F32)
    for s, v in zip(ss, vs):
        s0, s1 = _halves(s)
        p0 = jnp.exp((s0 - mb) * SCALE)
        p1 = jnp.exp((s1 - mb) * SCALE)
        lw = lw + (p0 + p1)
        p = jnp.concatenate([p0.astype(BF16), p1.astype(BF16)], axis=1)
        o_win = o_win + jnp.dot(p, v, preferred_element_type=F32)
    o_win = o_win / jnp.sum(lw, axis=1, keepdims=True)

    gates = _sigmoid(gt_ref[0, 0, 0] + gb_ref[0])
    for r in range(B_REP):
        rs = slice(r * TQ, (r + 1) * TQ)
        o = (gates[:, 3 * r:3 * r + 1] * o_cmp[rs]
             + gates[:, 3 * r + 1:3 * r + 2] * o_sel[rs]
             + gates[:, 3 * r + 2:3 * r + 3] * o_win[rs])
        y_ref[0, r] = (o * _silu(z_ref[0, 0, r])).astype(BF16)


def _attn_b(hb, hf, kvc, gate_b):
    _, B, _, S, _ = hb.shape
    TQ = B_TQ
    n_slc = S // SEL_LEN
    ncmp = S // CMP_STRIDE
    j = np.arange(ncmp)[None, :]
    n = np.arange(n_slc)[:, None]
    cover_t = ((j * CMP_STRIDE < (n + 1) * SEL_LEN) & (j * CMP_STRIDE + CMP_LEN > n * SEL_LEN)
               & (j < ncmp - 1)).astype(np.float32)
    onehot = (np.arange(S)[:, None] // SEL_LEN == np.arange(LANES)[None, :]).astype(np.float32)
    nq = B_REP * B_KV
    kv = lambda s0: pl.BlockSpec((1, 1, 1, S, LANES), lambda b, g, i, s0=s0: (0, b, s0 + g, 0, 0))
    return pl.pallas_call(
        functools.partial(_attn_b_kernel, S=S),
        grid=(B, B_KV, S // TQ),
        in_specs=[
            pl.BlockSpec((1, 1, B_REP, TQ, LANES), lambda b, g, i: (0, b, g, i, 0)),
            kv(nq), kv(nq + 2 * B_KV), kv(nq + B_KV), kv(nq + 3 * B_KV),
            pl.BlockSpec((1, 1, 1, ncmp, LANES), lambda b, g, i: (b, 0, g, 0, 0)),
            pl.BlockSpec((1, 1, 1, ncmp, LANES), lambda b, g, i: (b, 1, g, 0, 0)),
            pl.BlockSpec((1, 1, B_REP, TQ, LANES), lambda b, g, i: (0, b, 1 + g, i, 0)),
            pl.BlockSpec((1, 1, 1, TQ, LANES), lambda b, g, i: (0, b, 12 + g, i, 0)),
            pl.BlockSpec((1, 1, LANES), lambda b, g, i: (g, 0, 0)),
            pl.BlockSpec((n_slc, ncmp), lambda b, g, i: (0, 0)),
            pl.BlockSpec((S, LANES), lambda b, g, i: (0, 0)),
        ],
        out_specs=pl.BlockSpec((1, B_REP, TQ, LANES), lambda b, g, i: (b, g, i, 0)),
        out_shape=jax.ShapeDtypeStruct((B, nq, S, LANES), BF16),
        scratch_shapes=[pltpu.VMEM((S, 2 * HEAD_DIM), BF16),
                        pltpu.VMEM((S // TQ, B_REP * TQ, TQ), F32),
                        pltpu.VMEM((B_REP * TQ, LANES), F32),
                        pltpu.VMEM((B_REP * TQ, LANES), F32),
                        pltpu.VMEM((B_REP * TQ, HEAD_DIM), F32)],
        compiler_params=_cparams("parallel", "parallel", "arbitrary"),
        name="attn_b",
    )(hb, hb, hb, hb, hb, kvc, kvc, hf, hf, gate_b, jnp.asarray(cover_t, BF16), jnp.asarray(onehot, BF16))


def _mixer_b(x, gain, w_in, gate_b, pe_k, w1_k, w2_k, pe_v, w1_v, w2_v, w_out, tables, final_gain=None):
    B, S, D = x.shape
    bw = D
    kvw = B_KV * HEAD_DIM
    o0, o1, o2 = bw, bw + 6 * kvw, 2 * bw + 6 * kvw
    W = w_in.astype(BF16)
    kvcol = lambda c: W[:, o0 + c * kvw:o0 + (c + 1) * kvw]
    wg = W[:, o2:].reshape(D, B_KV, B_REP * 3)
    wg = jnp.pad(wg, ((0, 0), (0, 0), (0, LANES - B_REP * 3))).reshape(D, B_KV * LANES)
    gb = jnp.pad(gate_b.reshape(B_KV, 1, B_REP * 3), ((0, 0), (0, 0), (0, LANES - B_REP * 3)))
    w_b = [_rope_layout(W[:, :o0]), _rope_layout(kvcol(2)), _rope_layout(kvcol(4)), kvcol(3), kvcol(5)]
    w_f = [_rope_layout(kvcol(0)), kvcol(1), W[:, o1:o2], wg]
    parts = [((bw + 4 * kvw) // LANES, BF16, (bw + 2 * kvw) // LANES),
             ((2 * kvw + bw) // LANES + B_KV, F32, kvw // LANES)]
    hb, hf = _norm_proj(x, gain, jnp.concatenate(w_b + w_f, axis=1), parts, d=1, tables=tables)
    w1_kl = jnp.swapaxes(_rope_layout(jnp.swapaxes(w1_k, 1, 2)), 1, 2)
    kvc = _compress(hf[0, :, :2 * B_KV], _rope_layout(pe_k), w1_kl, _rope_layout(w2_k), pe_v, w1_v, w2_v)
    y = _attn_b(hb, hf, kvc, gb)
    return _out_proj(y, w_out.astype(BF16), x, final_gain)


def _rglru_kernel(xb_ref, z_ref, cw_ref, cb_ref, wa_ref, ba_ref, wx_ref, bx_ref, lam_ref, y_ref, xpad_ref, *, S):
    PAD = 8
    xpad_ref[0:PAD, :] = jnp.zeros((PAD, LANES), F32)
    xpad_ref[PAD:PAD + S, :] = xb_ref[0, 0, 0]
    nlam = -lam_ref[...]
    softplus = jnp.maximum(nlam, 0.0) + jnp.log(1.0 + jnp.exp(-jnp.abs(nlam)))
    row = lax.broadcasted_iota(jnp.int32, (C_CHUNK, LANES), 0)

    def chunk(c, h_prev):
        base = c * C_CHUNK
        xc = jnp.broadcast_to(cb_ref[...], (C_CHUNK, LANES))
        for k in range(CONV_W):
            xc = xc + cw_ref[k:k + 1, :] * xpad_ref[pl.ds(base + PAD - (CONV_W - 1) + k, C_CHUNK), :]
        xcb = xc.astype(BF16)
        r = _sigmoid(jnp.dot(xcb, wa_ref[0], preferred_element_type=F32) + ba_ref[...])
        i = _sigmoid(jnp.dot(xcb, wx_ref[0], preferred_element_type=F32) + bx_ref[...])
        log_a = -LRU_C * r * softplus
        a = jnp.exp(log_a)
        b = jnp.sqrt(1.0 - a * a) * i * xc
        k = 1
        while k < 8:
            keep = row >= k
            a_s = jnp.where(keep, pltpu.roll(a, k, 0), 1.0)
            b_s = jnp.where(keep, pltpu.roll(b, k, 0), 0.0)
            b = a * b_s + b
            a = a * a_s
            k *= 2
        while k < C_CHUNK:
            b = jnp.concatenate([b[:k], a[k:] * b[:-k] + b[k:]], axis=0)
            a = jnp.concatenate([a[:k], a[k:] * a[:-k]], axis=0)
            k *= 2
        h = b + a * h_prev
        rows = pl.ds(pl.multiple_of(base, C_CHUNK), C_CHUNK)
        y_ref[0, 0, rows, :] = (h * _silu(z_ref[0, 0, 0, rows, :])).astype(BF16)
        return h[C_CHUNK - 1:C_CHUNK, :]

    lax.fori_loop(0, S // C_CHUNK, chunk, jnp.zeros((1, LANES), F32))


def _rglru(h, conv_w, conv_b, wa, ba, wx, bx, lam):
    _, B, _, S, _ = h.shape
    nbk = C_BLOCKS
    vec = lambda v: v.reshape(1, nbk * LANES)
    vspec = pl.BlockSpec((1, LANES), lambda b, n: (0, n))
    mspec = pl.BlockSpec((1, LANES, LANES), lambda b, n: (n, 0, 0))
    return pl.pallas_call(
        functools.partial(_rglru_kernel, S=S),
        grid=(B, nbk),
        in_specs=[
            pl.BlockSpec((1, 1, 1, S, LANES), lambda b, n: (0, b, n, 0, 0)),
            pl.BlockSpec((1, 1, 1, S, LANES), lambda b, n: (0, b, nbk + n, 0, 0)),
            pl.BlockSpec((CONV_W, LANES), lambda b, n: (0, n)),
            vspec, mspec, vspec, mspec, vspec, vspec,
        ],
        out_specs=pl.BlockSpec((1, 1, S, LANES), lambda b, n: (b, n, 0, 0)),
        out_shape=jax.ShapeDtypeStruct((B, nbk, S, LANES), BF16),
        scratch_shapes=[pltpu.VMEM((S + 8, LANES), F32)],
        compiler_params=_cparams("parallel", "parallel"),
        name="rglru",
    )(h, h, conv_w, vec(conv_b), wa.astype(BF16), vec(ba), wx.astype(BF16), vec(bx), vec(lam))


def _mixer_c(x, gain, w_in, conv_w, conv_b, wa, ba, wx, bx, lam, w_out, final_gain=None):
    h, = _norm_proj(x, gain, w_in.astype(BF16), [(w_in.shape[1] // LANES, F32, 0)], d=1, tables=None)
    y = _rglru(h, conv_w, conv_b, wa, ba, wx, bx, lam)
    return _out_proj(y, w_out.astype(BF16), x, final_gain)


def kernel(x, norm_g, final_g, a_w_in, a_w_out, b_w_in, b_gate_b, b_pe_k, b_w1_k, b_w2_k, b_pe_v, b_w1_v,
           b_w2_v, b_w_out, c_w_in, c_conv_w, c_conv_b, c_wa, c_ba, c_wx, c_bx, c_lambda, c_w_out):
    depth = norm_g.shape[0]
    tables = _rope_tables(x.shape[1])
    for i in range(depth):
        kind, j = i % 3, i // 3
        final = final_g if i == depth - 1 else None
        if kind == 0:
            x = _mixer_a(x, norm_g[i], a_w_in[j], a_w_out[j], tables, final)
        elif kind == 1:
            x = _mixer_b(x, norm_g[i], b_w_in[j], b_gate_b[j], b_pe_k[j], b_w1_k[j], b_w2_k[j],
                         b_pe_v[j], b_w1_v[j], b_w2_v[j], b_w_out[j], tables, final)
        else:
            x = _mixer_c(x, norm_g[i], c_w_in[j], c_conv_w[j], c_conv_b[j], c_wa[j], c_ba[j],
                         c_wx[j], c_bx[j], c_lambda[j], c_w_out[j], final)
    return x
```

```python
import functools

import numpy as np
import jax
import jax.numpy as jnp
from jax import lax
from jax.experimental import pallas as pl
from jax.experimental.pallas import tpu as pltpu

F32 = jnp.float32
BF16 = jnp.bfloat16

HEAD_DIM = 128
LANES = 128
ROPE_DIM = HEAD_DIM // 4
ROPE_HALF = ROPE_DIM // 2
ROPE_THETA = 500000.0
NORM_EPS = 1e-6
SCALE = HEAD_DIM ** -0.5
NEG = -1e30
VMEM_LIMIT = 48 * 1024 * 1024

A_GROUPS = ((128, 1), (512, 4), (2048, 16))
A_BLK = 128
A_UNROLL = 8
B_KV = 2
B_REP = 4
CMP_LEN = 32
CMP_STRIDE = 16
SEL_LEN = 64
N_SELECT = 16
WIN = 512
FORCE_SCORE = 1000.0
B_TQ = 256
C_BLOCKS = 10
CONV_W = 4
LRU_C = 8.0
C_CHUNK = 256
C_GROUP = 32


def _cparams(*sem):
    return pltpu.CompilerParams(dimension_semantics=sem, vmem_limit_bytes=VMEM_LIMIT)


LOG2E = 1.4426950408889634


def _sigmoid(x):
    return 1.0 / (1.0 + jnp.exp2(x * -LOG2E))


def _silu(x):
    return x * _sigmoid(x)


ROPE_SWAP = LANES // 2


def _rope_layout(w):
    lead = w.shape[:-1]
    h = w.reshape(*lead, -1, HEAD_DIM)
    h = jnp.concatenate([h[..., :ROPE_HALF], h[..., ROPE_DIM:ROPE_SWAP + ROPE_HALF],
                         h[..., ROPE_HALF:ROPE_DIM], h[..., ROPE_SWAP + ROPE_HALF:]], axis=-1)
    return h.reshape(*lead, -1)


def _rope_tables(S):
    inv_freq = ROPE_THETA ** (-2.0 * np.arange(ROPE_HALF, dtype=np.float64) / ROPE_DIM)
    ang = np.arange(S, dtype=np.float64)[:, None] * inv_freq[None, :]
    cos, sin = np.cos(ang), np.sin(ang)
    gap = ROPE_SWAP - ROPE_HALF
    c = np.concatenate([cos, np.ones((S, gap)), cos, np.ones((S, gap))], axis=1).astype(np.float32)
    sn = np.concatenate([-sin, np.zeros((S, gap)), sin, np.zeros((S, gap))], axis=1).astype(np.float32)
    return c, sn


PROJ_TN = 512


def _dilated_row_copies(x_hbm, xbuf, sem, step, slot, *, d, nb, lt, n_mt):
    r = step // n_mt
    mi = step % n_mt
    return [pltpu.make_async_copy(x_hbm.at[mi * nb + bb, :, r, :],
                                  xbuf.at[slot, pl.ds(bb * lt, lt), :], sem.at[slot]) for bb in range(nb)]


def _norm_proj_kernel(*refs, nb, lt, parts, any_rope, gather):
    n_out = len(parts)
    if gather:
        xbuf, sem = refs[-2:]
        refs = refs[:-2]
        d, n_mt = gather
        step = pl.program_id(0) * n_mt + pl.program_id(1)
        slot = step % 2
        copies = functools.partial(_dilated_row_copies, refs[0], xbuf, sem, d=d, nb=nb, lt=lt, n_mt=n_mt)

        @pl.when(step == 0)
        def _():
            for cp in copies(0, 0):
                cp.start()

        @pl.when(step + 1 < d * n_mt)
        def _():
            for cp in copies(step + 1, 1 - slot):
                cp.start()

        for cp in copies(step, slot):
            cp.wait()
    o_refs, xn_ref = refs[-n_out - 1:-1], refs[-1]
    if any_rope:
        x_ref, g_ref, w_ref, c_ref, sn_ref = refs[:5]
    else:
        x_ref, g_ref, w_ref = refs[:3]
    xf = xbuf[slot] if gather else x_ref[...]
    ms = jnp.mean(xf * xf, axis=-1, keepdims=True)
    xn_ref[...] = (xf * lax.rsqrt(ms + NORM_EPS) * g_ref[...]).astype(BF16)
    col = 0
    for o_ref, (ns, _, n_rope) in zip(o_refs, parts):
        hs = PROJ_TN // LANES if ns % (PROJ_TN // LANES) == 0 else 2
        assert ns % hs == 0
        for c in range(ns // hs):
            tn = hs * LANES
            res = jnp.dot(xn_ref[...], w_ref[:, col:col + tn], preferred_element_type=F32)
            col += tn
            for bb in range(nb):
                for hh in range(hs):
                    sub = res[bb * lt:(bb + 1) * lt, hh * LANES:(hh + 1) * LANES]
                    if c * hs + hh < n_rope:
                        sub = sub * c_ref[...] + pltpu.roll(sub, ROPE_SWAP, 1) * sn_ref[...]
                    o_ref[0, bb, c * hs + hh] = sub.astype(o_ref.dtype)


def _norm_proj(x, gain, w, parts, *, d, tables, tm=512):
    B, S, D = x.shape
    N = w.shape[1]
    L = S // d
    rows = B * L
    tm = min(tm, rows)
    assert N == LANES * sum(p[0] for p in parts) and rows % tm == 0
    assert (tm % L == 0) or (L % tm == 0)
    if tm >= L:
        nb, lt = tm // L, L
        o_spec = lambda ns: pl.BlockSpec((1, nb, ns, L, LANES), lambda r, mi: (r, mi, 0, 0, 0))
        t_map = lambda r, mi: (0, r)
    else:
        nb, lt, tps = 1, tm, L // tm
        o_spec = lambda ns: pl.BlockSpec((1, 1, ns, tm, LANES), lambda r, mi: (r, mi // tps, 0, mi % tps, 0))
        t_map = lambda r, mi: (mi % tps, r)
    gather = (d, rows // tm) if d > 1 else None
    if gather:
        assert tm >= L
        x_spec, x_arg = pl.BlockSpec(memory_space=pl.ANY), x.reshape(B, L, d, D)
        scratch = [pltpu.VMEM((2, tm, D), F32), pltpu.SemaphoreType.DMA((2,))]
        sem = ("arbitrary", "arbitrary")
    else:
        x_spec, x_arg = pl.BlockSpec((tm, D), lambda r, mi: (mi, r)), x.reshape(rows, D)
        scratch = []
        sem = ("parallel", "parallel")
    in_specs = [
        x_spec,
        pl.BlockSpec((1, D), lambda r, mi: (0, 0)),
        pl.BlockSpec((D, N), lambda r, mi: (0, 0)),
    ]
    args = [x_arg, gain.reshape(1, D), w]
    any_rope = any(p[2] for p in parts)
    if any_rope:
        in_specs += [pl.BlockSpec((lt, LANES), t_map)] * 2
        args += [t.reshape(L, d * LANES) for t in tables]
    return pl.pallas_call(
        functools.partial(_norm_proj_kernel, nb=nb, lt=lt, parts=tuple(parts), any_rope=any_rope, gather=gather),
        grid=(d, rows // tm),
        in_specs=in_specs,
        out_specs=[o_spec(p[0]) for p in parts],
        out_shape=[jax.ShapeDtypeStruct((d, B, p[0], L, LANES), p[1]) for p in parts],
        scratch_shapes=[pltpu.VMEM((tm, D), BF16)] + scratch,
        compiler_params=_cparams(*sem),
        name=f"norm_proj_d{d}_n{N}",
    )(*args)


def _out_proj_kernel(*refs, nh, final):
    if final:
        y_ref, w_ref, x_ref, g_ref, o_ref = refs
    else:
        y_ref, w_ref, x_ref, o_ref = refs
    y = jnp.concatenate([y_ref[0, h] for h in range(nh)], axis=1)
    xn = x_ref[0] + jnp.dot(y, w_ref[...], preferred_element_type=F32)
    if final:
        ms = jnp.mean(xn * xn, axis=-1, keepdims=True)
        xn = xn * lax.rsqrt(ms + NORM_EPS) * g_ref[...]
    o_ref[0] = xn


def _out_proj(y, w, x, final_gain=None, tm=512):
    B, nh, S, _ = y.shape
    D = x.shape[-1]
    final = final_gain is not None
    in_specs = [
        pl.BlockSpec((1, nh, tm, LANES), lambda b, i: (b, 0, i, 0)),
        pl.BlockSpec((nh * LANES, D), lambda b, i: (0, 0)),
        pl.BlockSpec((1, tm, D), lambda b, i: (b, i, 0)),
    ]
    args = [y, w, x]
    if final:
        in_specs.append(pl.BlockSpec((1, D), lambda b, i: (0, 0)))
        args.append(final_gain.reshape(1, D))
    return pl.pallas_call(
        functools.partial(_out_proj_kernel, nh=nh, final=final),
        grid=(B, S // tm),
        in_specs=in_specs,
        out_specs=pl.BlockSpec((1, tm, D), lambda b, i: (b, i, 0)),
        out_shape=jax.ShapeDtypeStruct((B, S, D), F32),
        compiler_params=_cparams("parallel", "parallel"),
        name="out_proj",
    )(*args)


def _attn_a_kernel(q1, k1, v1, q2, k2, v2, q3, k3, v3, z_ref, y_ref, o_s, l_s, t_s, b_s, *, S):
    groups = ((q1, k1, v1), (q2, k2, v2), (q3, k3, v3))
    nblk = S // A_BLK
    PS = 4

    a = lax.broadcasted_iota(jnp.int32, (A_BLK, 2 * A_BLK), 0)
    c = lax.broadcasted_iota(jnp.int32, (A_BLK, 2 * A_BLK), 1)
    b_s[0] = jnp.where(c <= a, 0.0, NEG)
    b_s[1] = jnp.where((c >= a) & (c <= a + A_BLK), 0.0, NEG)

    for gi, ((q_ref, k_ref, v_ref), (win, d)) in enumerate(zip(groups, A_GROUPS)):
        L = S // d
        nqb = L // A_BLK
        nk = 2 * A_BLK if nqb > 1 else A_BLK
        assert win // d == A_BLK and d in (1, PS, PS * PS)

        def blk(tt, carry, q_ref=q_ref, k_ref=k_ref, v_ref=v_ref, d=d, nqb=nqb, nk=nk, gi=gi):
            idx, ss, vs = [], [], []
            for u in range(A_UNROLL):
                t = tt * A_UNROLL + u
                r = t // nqb
                i = t % nqb
                ks = jnp.maximum(i - 1, 0) * A_BLK
                q = q_ref[r, 0, 0, pl.ds(pl.multiple_of(i * A_BLK, A_BLK), A_BLK), :]
                k = k_ref[r, 0, 0, pl.ds(pl.multiple_of(ks, A_BLK), nk), :]
                vs.append(v_ref[r, 0, 0, pl.ds(pl.multiple_of(ks, A_BLK), nk), :])
                ss.append(lax.dot_general(q, k, (((1,), (1,)), ((), ())), preferred_element_type=F32))
                idx.append((t, r, i))
            ps, ms, ls = [], [], []
            for u in range(A_UNROLL):
                s = ss[u] * (SCALE * LOG2E) + b_s[jnp.minimum(idx[u][2], 1), :, 0:nk]
                m = jnp.max(s, axis=1, keepdims=True)
                p = jnp.exp2(s - m)
                ls.append(jnp.sum(p, axis=1, keepdims=True))
                ms.append(m)
                ps.append(p.astype(BF16))
            for u in range(A_UNROLL):
                t, r, i = idx[u]
                o = jnp.dot(ps[u], vs[u], preferred_element_type=F32) / ls[u]
                lse = jnp.broadcast_to(ms[u] + jnp.log2(ls[u]), (A_BLK, LANES))
                if d == 1:
                    rows = pl.ds(pl.multiple_of(t * A_BLK, A_BLK), A_BLK)
                    o_s[gi, rows, :] = o
                    l_s[gi, rows, :] = lse
                elif d == PS:
                    rows = pl.ds(i * (A_BLK * d) + r, A_BLK, stride=d)
                    o_s[gi, rows, :] = o
                    l_s[gi, rows, :] = lse
                else:
                    rows = pl.ds(r // PS, A_BLK, stride=PS)
                    t_s[0, r % PS, rows, :] = o
                    t_s[1, r % PS, rows, :] = lse
            return carry

        lax.fori_loop(0, nblk // A_UNROLL, blk, 0)
        if d == PS * PS:
            for r0 in range(PS):
                def second(cidx, carry, r0=r0, gi=gi):
                    src = pl.ds(pl.multiple_of(cidx * A_BLK, A_BLK), A_BLK)
                    dst = pl.ds(cidx * (A_BLK * PS) + r0, A_BLK, stride=PS)
                    o_s[gi, dst, :] = t_s[0, r0, src, :]
                    l_s[gi, dst, :] = t_s[1, r0, src, :]
                    return carry
                lax.fori_loop(0, S // PS // A_BLK, second, 0)

    def merge(c, carry):
        rows = pl.ds(pl.multiple_of(c * A_BLK, A_BLK), A_BLK)
        l1, l2, l3 = l_s[0, rows, :], l_s[1, rows, :], l_s[2, rows, :]
        mx = jnp.maximum(jnp.maximum(l1, l2), l3)
        e1, e2, e3 = jnp.exp2(l1 - mx), jnp.exp2(l2 - mx), jnp.exp2(l3 - mx)
        o = (e1 * o_s[0, rows, :] + e2 * o_s[1, rows, :] + e3 * o_s[2, rows, :]) / (e1 + e2 + e3)
        y_ref[0, 0, rows, :] = (o * _silu(z_ref[0, 0, 0, rows, :])).astype(BF16)
        return carry

    lax.fori_loop(0, nblk, merge, 0)


def _attn_a(qkv, z):
    _, B, nh, S, _ = z.shape
    in_specs, args = [], []
    for arr, (_, d) in zip(qkv, A_GROUPS):
        L = S // d
        for t in range(3):
            in_specs.append(pl.BlockSpec((d, 1, 1, L, LANES), lambda b, h, t=t, nh=nh: (0, b, t * nh + h, 0, 0)))
            args.append(arr)
    in_specs.append(pl.BlockSpec((1, 1, 1, S, LANES), lambda b, h: (0, b, h, 0, 0)))
    args.append(z)
    return pl.pallas_call(
        functools.partial(_attn_a_kernel, S=S),
        grid=(B, nh),
        in_specs=in_specs,
        out_specs=pl.BlockSpec((1, 1, S, LANES), lambda b, h: (b, h, 0, 0)),
        out_shape=jax.ShapeDtypeStruct((B, nh, S, LANES), BF16),
        scratch_shapes=[pltpu.VMEM((3, S, LANES), F32), pltpu.VMEM((3, S, LANES), F32),
                        pltpu.VMEM((2, 4, S // 4, LANES), F32), pltpu.VMEM((2, A_BLK, 2 * A_BLK), F32)],
        compiler_params=_cparams("parallel", "parallel"),
        name="attn_a",
    )(*args)


def _mixer_a(x, gain, w_in, w_out, tables, final_gain=None):
    B, S, D = x.shape
    W = w_in
    n_g = len(A_GROUPS)
    aw = D
    nh = aw // LANES
    qkv = []
    for g, (_, d) in enumerate(A_GROUPS):
        c0 = g * 3 * aw
        wg = [_rope_layout(W[:, c0:c0 + 2 * aw]), W[:, c0 + 2 * aw:c0 + 3 * aw]]
        parts = [(3 * nh, BF16, 2 * nh)]
        if d == 1:
            wg.append(W[:, n_g * 3 * aw:])
            parts.append((nh, F32, 0))
        outs = _norm_proj(x, gain, jnp.concatenate(wg, axis=1).astype(BF16), parts, d=d, tables=tables)
        qkv.append(outs[0])
        if d == 1:
            z = outs[1]
    y = _attn_a(qkv, z)
    return _out_proj(y, w_out.astype(BF16), x, final_gain)


def _compress_kernel(k_ref, v_ref, pek_ref, pev_ref, w1k_ref, w1v_ref, w2k_ref, w2v_ref, o_ref):
    def one(c_ref, pe_ref, w1_ref, w2_ref):
        half = c_ref.shape[-1]
        ch = c_ref[0, 0]
        u = jnp.dot((ch + pe_ref[0:1, :]).astype(BF16), w1_ref[0:half, :], preferred_element_type=F32)
        v = jnp.dot((ch + pe_ref[1:2, :]).astype(BF16), w1_ref[half:2 * half, :], preferred_element_type=F32)
        h = _silu(u + pltpu.roll(v, v.shape[0] - 1, 0))
        return jnp.dot(h.astype(BF16), w2_ref[...], preferred_element_type=F32).astype(BF16)

    o_ref[0, 0, 0] = one(k_ref, pek_ref, w1k_ref, w2k_ref)
    o_ref[0, 1, 0] = one(v_ref, pev_ref, w1v_ref, w2v_ref)


def _compress(cmp_arr, pe_k, w1_k, w2_k, pe_v, w1_v, w2_v):
    B, ns, S, _ = cmp_arr.shape
    nch = S // CMP_STRIDE
    cw = CMP_STRIDE * HEAD_DIM
    chunks = cmp_arr.reshape(B, ns, nch, cw)
    pe = lambda p: p.reshape(2, cw)
    w1 = lambda w: w.reshape(CMP_LEN * HEAD_DIM, -1).astype(BF16)
    const = lambda shape: pl.BlockSpec(shape, lambda b, g: (0,) * len(shape))
    return pl.pallas_call(
        _compress_kernel,
        grid=(B, B_KV),
        in_specs=[
            pl.BlockSpec((1, 1, nch, cw), lambda b, g: (b, g, 0, 0)),
            pl.BlockSpec((1, 1, nch, cw), lambda b, g: (b, B_KV + g, 0, 0)),
            const((2, cw)), const((2, cw)),
            const((CMP_LEN * HEAD_DIM, HEAD_DIM)), const((CMP_LEN * HEAD_DIM, HEAD_DIM)),
            const((HEAD_DIM, HEAD_DIM)), const((HEAD_DIM, HEAD_DIM)),
        ],
        out_specs=pl.BlockSpec((1, 2, 1, nch, HEAD_DIM), lambda b, g: (b, 0, g, 0, 0)),
        out_shape=jax.ShapeDtypeStruct((B, 2, B_KV, nch, HEAD_DIM), BF16),
        compiler_params=_cparams("parallel", "parallel"),
        name="compress",
    )(chunks, chunks, pe(pe_k), pe(pe_v), w1(w1_k), w1(w1_v), w2_k.astype(BF16), w2_v.astype(BF16))


def _halves(x):
    return x[:, 0:LANES], x[:, LANES:2 * LANES]


def _attn_b_kernel(q_ref, ksel_ref, vsel_ref, kwin_ref, vwin_ref, kc_ref, vc_ref, z_ref, gt_ref, gb_ref,
                   cov_ref, oh_ref, y_ref, kaug_ref, s_ref, mp_ref, lp_ref, acc_ref, *, S):
    TQ = B_TQ
    R = B_REP * TQ
    n_slc = S // SEL_LEN
    ncmp = S // CMP_STRIDE
    qi = pl.program_id(2)
    t0 = qi * TQ

    @pl.when(qi == 0)
    def _():
        kaug_ref[:, 0:HEAD_DIM] = ksel_ref[0, 0, 0]
        kaug_ref[:, HEAD_DIM:2 * HEAD_DIM] = oh_ref[...]

    q = q_ref[0, 0].reshape(R, HEAD_DIM)
    kc = kc_ref[0, 0, 0]
    vc = vc_ref[0, 0, 0]
    dn_t = (((1,), (1,)), ((), ()))

    s = lax.dot_general(q, kc, dn_t, preferred_element_type=F32) * SCALE
    tq = t0 + lax.broadcasted_iota(jnp.int32, (R, ncmp), 0) % TQ
    blk_end = lax.broadcasted_iota(jnp.int32, (R, ncmp), 1) * CMP_STRIDE + (CMP_LEN - 1)
    cmask = blk_end <= tq
    sm = jnp.where(cmask, s, NEG)
    m = jnp.max(sm, axis=1, keepdims=True)
    e = jnp.where(cmask, jnp.exp(sm - m), 0.0)
    den = jnp.sum(e, axis=1, keepdims=True)
    o_cmp = jnp.dot(e.astype(BF16), vc, preferred_element_type=F32) / jnp.maximum(den, 1e-30)

    st = lax.dot_general(kc, q, dn_t, preferred_element_type=F32) * SCALE
    tq_t = t0 + lax.broadcasted_iota(jnp.int32, (ncmp, R), 1) % TQ
    blk_end_t = lax.broadcasted_iota(jnp.int32, (ncmp, R), 0) * CMP_STRIDE + (CMP_LEN - 1)
    cmask_t = blk_end_t <= tq_t
    smt = jnp.where(cmask_t, st, NEG)
    mt = jnp.max(smt, axis=0, keepdims=True)
    et = jnp.where(cmask_t, jnp.exp(smt - mt), 0.0)
    pt = et / jnp.maximum(jnp.sum(et, axis=0, keepdims=True), 1e-30)
    psum = pt[:, 0:TQ]
    for r in range(1, B_REP):
        psum = psum + pt[:, r * TQ:(r + 1) * TQ]
    p_hi = psum.astype(BF16)
    p_lo = (psum - p_hi.astype(F32)).astype(BF16)
    cov = cov_ref[...]
    p_slc = (jnp.dot(cov, p_hi, preferred_element_type=F32)
             + jnp.dot(cov, p_lo, preferred_element_type=F32))
    nblk = lax.broadcasted_iota(jnp.int32, (n_slc, TQ), 0)
    cur = (t0 + lax.broadcasted_iota(jnp.int32, (n_slc, TQ), 1)) // SEL_LEN
    forced = (nblk == 0) | (nblk == cur) | (nblk == cur - 1)
    allowed = nblk <= cur
    score = jnp.where(allowed, jnp.where(forced, FORCE_SCORE, p_slc), -jnp.inf)
    rank = jnp.zeros((n_slc, TQ), F32)
    for mblk in range(n_slc):
        row = score[mblk:mblk + 1, :]
        beats = (row > score) | ((row == score) & (nblk > mblk))
        rank = rank + jnp.where(beats, 1.0, 0.0)
    sel_t = jnp.where((rank < N_SELECT) & allowed, 1.0, 0.0)
    sel_pad = jnp.concatenate([sel_t, jnp.zeros((LANES - n_slc, TQ), F32)], axis=0)
    sel = sel_pad.T
    lane = lax.broadcasted_iota(jnp.int32, (TQ, LANES), 1)
    bias = jnp.where((lane < n_slc) & (sel < 0.5), NEG, 0.0).astype(BF16)
    qaug = jnp.concatenate([q, jnp.concatenate([bias] * B_REP, axis=0)], axis=1)

    row_t = lax.broadcasted_iota(jnp.int32, (R, TQ), 0) % TQ
    col_t = lax.broadcasted_iota(jnp.int32, (R, TQ), 1)
    rows_d = pl.ds(pl.multiple_of(t0, TQ), TQ)

    def sel_scores(c, nch):
        rows = pl.ds(pl.multiple_of(c * TQ, TQ), nch * TQ)
        s = lax.dot_general(qaug, kaug_ref[rows, :], dn_t, preferred_element_type=F32)
        mx = mp_ref[...]
        for u in range(nch):
            su = s[:, u * TQ:(u + 1) * TQ]
            s_ref[c + u] = su
            s0, s1 = _halves(su)
            mx = jnp.maximum(mx, jnp.maximum(s0, s1))
        mp_ref[...] = mx

    def sel_scores_pair(c2, carry):
        sel_scores(2 * c2, 2)
        return carry

    mp_ref[...] = jnp.full((R, LANES), NEG, F32)
    lax.fori_loop(0, qi // 2, sel_scores_pair, 0)
    pl.when(qi % 2 == 1)(lambda: sel_scores(qi - 1, 1))
    s = lax.dot_general(qaug, kaug_ref[rows_d, :], dn_t, preferred_element_type=F32)
    s = jnp.where(col_t <= row_t, s, NEG)
    s_ref[qi] = s
    s0, s1 = _halves(s)
    m = jnp.max(jnp.maximum(mp_ref[...], jnp.maximum(s0, s1)), axis=1, keepdims=True)
    mp_ref[...] = jnp.broadcast_to(m, (R, LANES))
    lp_ref[...] = jnp.zeros((R, LANES), F32)
    acc_ref[...] = jnp.zeros((R, HEAD_DIM), F32)

    def sel_values(c, nch):
        rows = pl.ds(pl.multiple_of(c * TQ, TQ), nch * TQ)
        mb = mp_ref[...]
        lsum = lp_ref[...]
        ps = []
        for u in range(nch):
            s0, s1 = _halves(s_ref[c + u])
            p0 = jnp.exp2((s0 - mb) * (SCALE * LOG2E))
            p1 = jnp.exp2((s1 - mb) * (SCALE * LOG2E))
            lsum = lsum + (p0 + p1)
            ps += [p0.astype(BF16), p1.astype(BF16)]
        lp_ref[...] = lsum
        acc_ref[...] += jnp.dot(jnp.concatenate(ps, axis=1), vsel_ref[0, 0, 0, rows, :],
                                preferred_element_type=F32)

    def sel_values_pair(c2, carry):
        sel_values(2 * c2, 2)
        return carry

    lax.fori_loop(0, (qi + 1) // 2, sel_values_pair, 0)
    pl.when(qi % 2 == 0)(lambda: sel_values(qi, 1))
    o_sel = acc_ref[...] / jnp.sum(lp_ref[...], axis=1, keepdims=True)

    nprev = -(-(WIN - 1) // TQ)
    ss, vs = [], []
    for j in range(nprev + 1):
        cc = qi - nprev + j
        rows = pl.ds(pl.multiple_of(jnp.maximum(cc, 0) * TQ, TQ), TQ)
        s = lax.dot_general(q, kwin_ref[0, 0, 0, rows, :], dn_t, preferred_element_type=F32)
        dist = (nprev - j) * TQ + row_t - col_t
        if j == nprev:
            s = jnp.where(dist >= 0, s, NEG)
        else:
            if (nprev - j) * TQ + TQ - 1 > WIN - 1:
                s = jnp.where(dist <= WIN - 1, s, NEG)
            s = s + jnp.where(cc >= 0, 0.0, NEG)
        ss.append(s)
        vs.append(vwin_ref[0, 0, 0, rows, :])
    mw = ss[0]
    for s in ss[1:]:
        mw = jnp.maximum(mw, s)
    mw0, mw1 = _halves(mw)
    m = jnp.max(jnp.maximum(mw0, mw1), axis=1, keepdims=True)
    mb = jnp.broadcast_to(m, (R, LANES))
    lw = jnp.zeros((R, LANES), F32)
    o_win = jnp.zeros((R, HEAD_DIM), F32)
    for s, v in zip(ss, vs):
        s0, s1 = _halves(s)
        p0 = jnp.exp2((s0 - mb) * (SCALE * LOG2E))
        p1 = jnp.exp2((s1 - mb) * (SCALE * LOG2E))
        lw = lw + (p0 + p1)
        p = jnp.concatenate([p0.astype(BF16), p1.astype(BF16)], axis=1)
        o_win = o_win + jnp.dot(p, v, preferred_element_type=F32)
    o_win = o_win / jnp.sum(lw, axis=1, keepdims=True)

    gates = _sigmoid(gt_ref[0, 0, 0] + gb_ref[0])
    for r in range(B_REP):
        rs = slice(r * TQ, (r + 1) * TQ)
        o = (gates[:, 3 * r:3 * r + 1] * o_cmp[rs]
             + gates[:, 3 * r + 1:3 * r + 2] * o_sel[rs]
             + gates[:, 3 * r + 2:3 * r + 3] * o_win[rs])
        y_ref[0, r] = (o * _silu(z_ref[0, 0, r])).astype(BF16)


def _attn_b(hb, hf, kvc, gate_b):
    _, B, _, S, _ = hb.shape
    TQ = B_TQ
    n_slc = S // SEL_LEN
    ncmp = S // CMP_STRIDE
    j = np.arange(ncmp)[None, :]
    n = np.arange(n_slc)[:, None]
    cover_t = ((j * CMP_STRIDE < (n + 1) * SEL_LEN) & (j * CMP_STRIDE + CMP_LEN > n * SEL_LEN)
               & (j < ncmp - 1)).astype(np.float32)
    onehot = (np.arange(S)[:, None] // SEL_LEN == np.arange(LANES)[None, :]).astype(np.float32)
    nq = B_REP * B_KV
    kv = lambda s0: pl.BlockSpec((1, 1, 1, S, LANES), lambda b, g, i, s0=s0: (0, b, s0 + g, 0, 0))
    return pl.pallas_call(
        functools.partial(_attn_b_kernel, S=S),
        grid=(B, B_KV, S // TQ),
        in_specs=[
            pl.BlockSpec((1, 1, B_REP, TQ, LANES), lambda b, g, i: (0, b, g, i, 0)),
            kv(nq), kv(nq + 2 * B_KV), kv(nq + B_KV), kv(nq + 3 * B_KV),
            pl.BlockSpec((1, 1, 1, ncmp, LANES), lambda b, g, i: (b, 0, g, 0, 0)),
            pl.BlockSpec((1, 1, 1, ncmp, LANES), lambda b, g, i: (b, 1, g, 0, 0)),
            pl.BlockSpec((1, 1, B_REP, TQ, LANES), lambda b, g, i: (0, b, 1 + g, i, 0)),
            pl.BlockSpec((1, 1, 1, TQ, LANES), lambda b, g, i: (0, b, 12 + g, i, 0)),
            pl.BlockSpec((1, 1, LANES), lambda b, g, i: (g, 0, 0)),
            pl.BlockSpec((n_slc, ncmp), lambda b, g, i: (0, 0)),
            pl.BlockSpec((S, LANES), lambda b, g, i: (0, 0)),
        ],
        out_specs=pl.BlockSpec((1, B_REP, TQ, LANES), lambda b, g, i: (b, g, i, 0)),
        out_shape=jax.ShapeDtypeStruct((B, nq, S, LANES), BF16),
        scratch_shapes=[pltpu.VMEM((S, 2 * HEAD_DIM), BF16),
                        pltpu.VMEM((S // TQ, B_REP * TQ, TQ), F32),
                        pltpu.VMEM((B_REP * TQ, LANES), F32),
                        pltpu.VMEM((B_REP * TQ, LANES), F32),
                        pltpu.VMEM((B_REP * TQ, HEAD_DIM), F32)],
        compiler_params=_cparams("parallel", "parallel", "arbitrary"),
        name="attn_b",
    )(hb, hb, hb, hb, hb, kvc, kvc, hf, hf, gate_b, jnp.asarray(cover_t, BF16), jnp.asarray(onehot, BF16))


def _mixer_b(x, gain, w_in, gate_b, pe_k, w1_k, w2_k, pe_v, w1_v, w2_v, w_out, tables, final_gain=None):
    B, S, D = x.shape
    bw = D
    kvw = B_KV * HEAD_DIM
    o0, o1, o2 = bw, bw + 6 * kvw, 2 * bw + 6 * kvw
    W = w_in
    kvcol = lambda c: W[:, o0 + c * kvw:o0 + (c + 1) * kvw]
    wg = W[:, o2:].reshape(D, B_KV, B_REP * 3)
    wg = jnp.pad(wg, ((0, 0), (0, 0), (0, LANES - B_REP * 3))).reshape(D, B_KV * LANES)
    gb = jnp.pad(gate_b.reshape(B_KV, 1, B_REP * 3), ((0, 0), (0, 0), (0, LANES - B_REP * 3)))
    w_b = [_rope_layout(W[:, :o0]), _rope_layout(kvcol(2)), _rope_layout(kvcol(4)), kvcol(3), kvcol(5)]
    w_f = [_rope_layout(kvcol(0)), kvcol(1), W[:, o1:o2], wg]
    parts = [((bw + 4 * kvw) // LANES, BF16, (bw + 2 * kvw) // LANES),
             ((2 * kvw + bw) // LANES + B_KV, F32, kvw // LANES)]
    hb, hf = _norm_proj(x, gain, jnp.concatenate(w_b + w_f, axis=1).astype(BF16), parts, d=1, tables=tables)
    w1_kl = jnp.swapaxes(_rope_layout(jnp.swapaxes(w1_k, 1, 2)), 1, 2)
    kvc = _compress(hf[0, :, :2 * B_KV], _rope_layout(pe_k), w1_kl, _rope_layout(w2_k), pe_v, w1_v, w2_v)
    y = _attn_b(hb, hf, kvc, gb)
    return _out_proj(y, w_out.astype(BF16), x, final_gain)


def _rglru_kernel(xb_ref, z_ref, cw_ref, cb_ref, wa_ref, ba_ref, wx_ref, bx_ref, lam_ref, y_ref, xpad_ref, *, S):
    PAD = 8
    xpad_ref[0:PAD, :] = jnp.zeros((PAD, LANES), F32)
    xpad_ref[PAD:PAD + S, :] = xb_ref[0, 0, 0]
    nlam = -lam_ref[...]
    softplus = jnp.maximum(nlam, 0.0) + jnp.log(1.0 + jnp.exp(-jnp.abs(nlam)))
    row = lax.broadcasted_iota(jnp.int32, (C_CHUNK, LANES), 0)

    def chunk(c, h_prev):
        base = c * C_CHUNK
        xc = jnp.broadcast_to(cb_ref[...], (C_CHUNK, LANES))
        for k in range(CONV_W):
            xc = xc + cw_ref[k:k + 1, :] * xpad_ref[pl.ds(base + PAD - (CONV_W - 1) + k, C_CHUNK), :]
        xcb = xc.astype(BF16)
        r = _sigmoid(jnp.dot(xcb, wa_ref[0], preferred_element_type=F32) + ba_ref[...])
        i = _sigmoid(jnp.dot(xcb, wx_ref[0], preferred_element_type=F32) + bx_ref[...])
        log_a = -LRU_C * r * softplus
        a = jnp.exp(log_a)
        b = jnp.sqrt(1.0 - a * a) * i * xc
        k = 1
        while k < C_GROUP:
            keep = row % C_GROUP >= k
            a_s = jnp.where(keep, pltpu.roll(a, k, 0), 1.0)
            b_s = jnp.where(keep, pltpu.roll(b, k, 0), 0.0)
            b = a * b_s + b
            a = a * a_s
            k *= 2
        hs = []
        for g in range(C_CHUNK // C_GROUP):
            rows = slice(g * C_GROUP, (g + 1) * C_GROUP)
            hs.append(b[rows] + a[rows] * h_prev)
            h_prev = hs[-1][C_GROUP - 1:C_GROUP, :]
        h = jnp.concatenate(hs, axis=0)
        rows = pl.ds(pl.multiple_of(base, C_CHUNK), C_CHUNK)
        y_ref[0, 0, rows, :] = (h * _silu(z_ref[0, 0, 0, rows, :])).astype(BF16)
        return h[C_CHUNK - 1:C_CHUNK, :]

    lax.fori_loop(0, S // C_CHUNK, chunk, jnp.zeros((1, LANES), F32))


def _rglru(h, conv_w, conv_b, wa, ba, wx, bx, lam):
    _, B, _, S, _ = h.shape
    nbk = C_BLOCKS
    vec = lambda v: v.reshape(1, nbk * LANES)
    vspec = pl.BlockSpec((1, LANES), lambda b, n: (0, n))
    mspec = pl.BlockSpec((1, LANES, LANES), lambda b, n: (n, 0, 0))
    return pl.pallas_call(
        functools.partial(_rglru_kernel, S=S),
        grid=(B, nbk),
        in_specs=[
            pl.BlockSpec((1, 1, 1, S, LANES), lambda b, n: (0, b, n, 0, 0)),
            pl.BlockSpec((1, 1, 1, S, LANES), lambda b, n: (0, b, nbk + n, 0, 0)),
            pl.BlockSpec((CONV_W, LANES), lambda b, n: (0, n)),
            vspec, mspec, vspec, mspec, vspec, vspec,
        ],
        out_specs=pl.BlockSpec((1, 1, S, LANES), lambda b, n: (b, n, 0, 0)),
        out_shape=jax.ShapeDtypeStruct((B, nbk, S, LANES), BF16),
        scratch_shapes=[pltpu.VMEM((S + 8, LANES), F32)],
        compiler_params=_cparams("parallel", "parallel"),
        name="rglru",
    )(h, h, conv_w, vec(conv_b), wa.astype(BF16), vec(ba), wx.astype(BF16), vec(bx), vec(lam))


def _mixer_c(x, gain, w_in, conv_w, conv_b, wa, ba, wx, bx, lam, w_out, final_gain=None):
    h, = _norm_proj(x, gain, w_in.astype(BF16), [(w_in.shape[1] // LANES, F32, 0)], d=1, tables=None)
    y = _rglru(h, conv_w, conv_b, wa, ba, wx, bx, lam)
    return _out_proj(y, w_out.astype(BF16), x, final_gain)


def kernel(x, norm_g, final_g, a_w_in, a_w_out, b_w_in, b_gate_b, b_pe_k, b_w1_k, b_w2_k, b_pe_v, b_w1_v,
           b_w2_v, b_w_out, c_w_in, c_conv_w, c_conv_b, c_wa, c_ba, c_wx, c_bx, c_lambda, c_w_out):
    depth = norm_g.shape[0]
    tables = _rope_tables(x.shape[1])
    for i in range(depth):
        kind, j = i % 3, i // 3
        final = final_g if i == depth - 1 else None
        if kind == 0:
            x = _mixer_a(x, norm_g[i], a_w_in[j], a_w_out[j], tables, final)
        elif kind == 1:
            x = _mixer_b(x, norm_g[i], b_w_in[j], b_gate_b[j], b_pe_k[j], b_w1_k[j], b_w2_k[j],
                         b_pe_v[j], b_w1_v[j], b_w2_v[j], b_w_out[j], tables, final)
        else:
            x = _mixer_c(x, norm_g[i], c_w_in[j], c_conv_w[j], c_conv_b[j], c_wa[j], c_ba[j],
                         c_wx[j], c_bx[j], c_lambda[j], c_w_out[j], final)
    return x
```

```python
import functools

import numpy as np
import jax
import jax.numpy as jnp
from jax import lax
from jax.experimental import pallas as pl
from jax.experimental.pallas import tpu as pltpu

F32 = jnp.float32
BF16 = jnp.bfloat16

HEAD_DIM = 128
LANES = 128
SUBLANES = 8
ROPE_DIM = HEAD_DIM // 4
ROPE_HALF = ROPE_DIM // 2
ROPE_THETA = 500000.0
NORM_EPS = 1e-6
SCALE = HEAD_DIM ** -0.5
NEG = -1e30
VMEM_LIMIT = 48 * 1024 * 1024

A_GROUPS = ((128, 1), (512, 4), (2048, 16))
A_BLK = 128
A_UNROLL = 8
B_KV = 2
B_REP = 4
CMP_LEN = 32
CMP_STRIDE = 16
SEL_LEN = 64
N_SELECT = 16
WIN = 512
FORCE_SCORE = 1000.0
B_TQ = 256
C_BLOCKS = 10
CONV_W = 4
LRU_C = 8.0
C_CHUNK = 256
C_GROUP = 32


def _cparams(*sem):
    return pltpu.CompilerParams(dimension_semantics=sem, vmem_limit_bytes=VMEM_LIMIT)


LOG2E = 1.4426950408889634


def _sigmoid(x):
    return 1.0 / (1.0 + jnp.exp2(x * -LOG2E))


def _silu(x):
    return x * _sigmoid(x)


ROPE_SWAP = LANES // 2


def _rope_layout(w):
    lead = w.shape[:-1]
    h = w.reshape(*lead, -1, HEAD_DIM)
    h = jnp.concatenate([h[..., :ROPE_HALF], h[..., ROPE_DIM:ROPE_SWAP + ROPE_HALF],
                         h[..., ROPE_HALF:ROPE_DIM], h[..., ROPE_SWAP + ROPE_HALF:]], axis=-1)
    return h.reshape(*lead, -1)


def _rope_tables(S):
    inv_freq = ROPE_THETA ** (-2.0 * np.arange(ROPE_HALF, dtype=np.float64) / ROPE_DIM)
    ang = np.arange(S, dtype=np.float64)[:, None] * inv_freq[None, :]
    cos, sin = np.cos(ang), np.sin(ang)
    gap = ROPE_SWAP - ROPE_HALF
    c = np.concatenate([cos, np.ones((S, gap)), cos, np.ones((S, gap))], axis=1).astype(np.float32)
    sn = np.concatenate([-sin, np.zeros((S, gap)), sin, np.zeros((S, gap))], axis=1).astype(np.float32)
    return c, sn


PROJ_TN = 512


def _dilated_row_copies(x_hbm, xbuf, sem, step, slot, *, d, split, nb, lt, n_mt):
    r = step // n_mt
    mi = step % n_mt
    ls = lt // split
    return [pltpu.make_async_copy(x_hbm.at[mi * nb + bb, :, r + d * j, :],
                                  xbuf.at[slot, pl.ds(bb * lt + j * ls, ls), :], sem.at[slot])
            for bb in range(nb) for j in range(split)]


def _norm_proj_kernel(*refs, nb, lt, parts, any_rope, gather):
    n_out = len(parts)
    split = 1
    if any(p[3] for p in parts):
        cstage = refs[-1]
        refs = refs[:-1]
    if gather:
        d, split, n_mt = gather
        if split > 1:
            stage = refs[-1]
            refs = refs[:-1]
        xbuf, sem = refs[-2:]
        refs = refs[:-2]
        step = pl.program_id(0) * n_mt + pl.program_id(1)
        slot = step % 2
        copies = functools.partial(_dilated_row_copies, refs[0], xbuf, sem, d=d, split=split, nb=nb, lt=lt,
                                   n_mt=n_mt)

        @pl.when(step == 0)
        def _():
            for cp in copies(0, 0):
                cp.start()

        @pl.when(step + 1 < d * n_mt)
        def _():
            for cp in copies(step + 1, 1 - slot):
                cp.start()

        for cp in copies(step, slot):
            cp.wait()
    o_refs, xn_ref = refs[-n_out - 1:-1], refs[-1]
    if any_rope:
        x_ref, g_ref, w_ref, c_ref, sn_ref = refs[:5]
    else:
        x_ref, g_ref, w_ref = refs[:3]
    xf = xbuf[slot] if gather else x_ref[...]
    ms = jnp.mean(xf * xf, axis=-1, keepdims=True)
    xn_ref[...] = (xf * lax.rsqrt(ms + NORM_EPS) * g_ref[...]).astype(BF16)
    col = 0
    for o_ref, (ns, _, n_rope, ck) in zip(o_refs, parts):
        hs = PROJ_TN // LANES if ns % (PROJ_TN // LANES) == 0 else 2
        assert ns % hs == 0
        for c in range(ns // hs):
            tn = hs * LANES
            res = jnp.dot(xn_ref[...], w_ref[:, col:col + tn], preferred_element_type=F32)
            col += tn
            for bb in range(nb):
                for hh in range(hs):
                    sub = res[bb * lt:(bb + 1) * lt, hh * LANES:(hh + 1) * LANES]
                    if c * hs + hh < n_rope:
                        sub = sub * c_ref[...] + pltpu.roll(sub, ROPE_SWAP, 1) * sn_ref[...]
                    if split > 1:
                        ls = lt // split
                        for j in range(split):
                            stage[pl.ds(j, ls, stride=split), :] = sub[j * ls:(j + 1) * ls]
                        sub = stage[...]
                    if ck:
                        cstage[...] = sub
                        for p in range(ck):
                            o_ref[0, bb, c * hs + hh, :, p * LANES:(p + 1) * LANES] = (
                                cstage[pl.ds(p, lt // ck, stride=ck), :].astype(o_ref.dtype))
                    else:
                        o_ref[0, bb, c * hs + hh] = sub.astype(o_ref.dtype)


def _norm_proj(x, gain, w, parts, *, d, tables, tm=512):
    B, S, D = x.shape
    N = w.shape[1]
    L = S // d
    rows = B * L
    tm = min(tm, rows)
    assert N == LANES * sum(p[0] for p in parts) and rows % tm == 0
    assert (tm % L == 0) or (L % tm == 0)
    parts = [tuple(p) + (0,) * (4 - len(p)) for p in parts]
    if tm >= L:
        nb, lt = tm // L, L
        o_spec = lambda ns, ck: pl.BlockSpec((1, nb, ns, L // ck, ck * LANES), lambda r, mi: (r, mi, 0, 0, 0))
        t_map = lambda r, mi: (0, r)
    else:
        nb, lt, tps = 1, tm, L // tm
        o_spec = lambda ns, ck: pl.BlockSpec((1, 1, ns, tm // ck, ck * LANES),
                                             lambda r, mi: (r, mi // tps, 0, mi % tps, 0))
        t_map = lambda r, mi: (mi % tps, r)
    split = max(1, SUBLANES // d)
    gather = (d, split, rows // tm) if d > 1 else None
    if gather:
        assert tm >= L and (d * split) % SUBLANES == 0
        x_spec, x_arg = pl.BlockSpec(memory_space=pl.ANY), x.reshape(B, L // split, d * split, D)
        scratch = [pltpu.VMEM((2, tm, D), F32), pltpu.SemaphoreType.DMA((2,))]
        if split > 1:
            scratch.append(pltpu.VMEM((lt, LANES), F32))
        sem = ("arbitrary", "arbitrary")
    else:
        x_spec, x_arg = pl.BlockSpec((tm, D), lambda r, mi: (mi, r)), x.reshape(rows, D)
        scratch = []
        sem = ("parallel", "parallel")
    in_specs = [
        x_spec,
        pl.BlockSpec((1, D), lambda r, mi: (0, 0)),
        pl.BlockSpec((D, N), lambda r, mi: (0, 0)),
    ]
    args = [x_arg, gain.reshape(1, D), w]
    any_rope = any(p[2] for p in parts)
    if any_rope:
        in_specs += [pl.BlockSpec((lt, LANES), t_map)] * 2
        for t in tables:
            t = t.reshape(L, d * LANES)
            if gather and split > 1:
                t = np.concatenate([t[j::split] for j in range(split)], axis=0)
            args.append(t)
    if any(p[3] for p in parts):
        scratch.append(pltpu.VMEM((lt, LANES), F32))
    return pl.pallas_call(
        functools.partial(_norm_proj_kernel, nb=nb, lt=lt, parts=tuple(parts), any_rope=any_rope, gather=gather),
        grid=(d, rows // tm),
        in_specs=in_specs,
        out_specs=[o_spec(p[0], max(p[3], 1)) for p in parts],
        out_shape=[jax.ShapeDtypeStruct((d, B, p[0], L // max(p[3], 1), max(p[3], 1) * LANES), p[1])
                   for p in parts],
        scratch_shapes=[pltpu.VMEM((tm, D), BF16)] + scratch,
        compiler_params=_cparams(*sem),
        name=f"norm_proj_d{d}_n{N}",
    )(*args)


def _out_proj_kernel(*refs, nh, final):
    if final:
        y_ref, w_ref, x_ref, g_ref, o_ref = refs
    else:
        y_ref, w_ref, x_ref, o_ref = refs
    y = jnp.concatenate([y_ref[0, h] for h in range(nh)], axis=1)
    xn = x_ref[0] + jnp.dot(y, w_ref[...], preferred_element_type=F32)
    if final:
        ms = jnp.mean(xn * xn, axis=-1, keepdims=True)
        xn = xn * lax.rsqrt(ms + NORM_EPS) * g_ref[...]
    o_ref[0] = xn


def _out_proj(y, w, x, final_gain=None, tm=512):
    B, nh, S, _ = y.shape
    D = x.shape[-1]
    final = final_gain is not None
    in_specs = [
        pl.BlockSpec((1, nh, tm, LANES), lambda b, i: (b, 0, i, 0)),
        pl.BlockSpec((nh * LANES, D), lambda b, i: (0, 0)),
        pl.BlockSpec((1, tm, D), lambda b, i: (b, i, 0)),
    ]
    args = [y, w, x]
    if final:
        in_specs.append(pl.BlockSpec((1, D), lambda b, i: (0, 0)))
        args.append(final_gain.reshape(1, D))
    return pl.pallas_call(
        functools.partial(_out_proj_kernel, nh=nh, final=final),
        grid=(B, S // tm),
        in_specs=in_specs,
        out_specs=pl.BlockSpec((1, tm, D), lambda b, i: (b, i, 0)),
        out_shape=jax.ShapeDtypeStruct((B, S, D), F32),
        compiler_params=_cparams("parallel", "parallel"),
        name="out_proj",
    )(*args)


def _attn_a_kernel(q1, k1, v1, q2, k2, v2, q3, k3, v3, z_ref, y_ref, o_s, l_s, t_s, b_s, *, S):
    groups = ((q1, k1, v1), (q2, k2, v2), (q3, k3, v3))
    nblk = S // A_BLK
    PS = 4

    a = lax.broadcasted_iota(jnp.int32, (A_BLK, 2 * A_BLK), 0)
    c = lax.broadcasted_iota(jnp.int32, (A_BLK, 2 * A_BLK), 1)
    b_s[0] = jnp.where(c <= a, 0.0, NEG)
    b_s[1] = jnp.where((c >= a) & (c <= a + A_BLK), 0.0, NEG)

    for gi, ((q_ref, k_ref, v_ref), (win, d)) in enumerate(zip(groups, A_GROUPS)):
        L = S // d
        nqb = L // A_BLK
        nk = 2 * A_BLK if nqb > 1 else A_BLK
        assert win // d == A_BLK and d in (1, PS, PS * PS)

        def blk(tt, carry, q_ref=q_ref, k_ref=k_ref, v_ref=v_ref, d=d, nqb=nqb, nk=nk, gi=gi):
            idx, ss, vs = [], [], []
            for u in range(A_UNROLL):
                t = tt * A_UNROLL + u
                r = t // nqb
                i = t % nqb
                ks = jnp.maximum(i - 1, 0) * A_BLK
                q = q_ref[r, 0, 0, pl.ds(pl.multiple_of(i * A_BLK, A_BLK), A_BLK), :]
                k = k_ref[r, 0, 0, pl.ds(pl.multiple_of(ks, A_BLK), nk), :]
                vs.append(v_ref[r, 0, 0, pl.ds(pl.multiple_of(ks, A_BLK), nk), :])
                ss.append(lax.dot_general(q, k, (((1,), (1,)), ((), ())), preferred_element_type=F32))
                idx.append((t, r, i))
            ps, ms, ls = [], [], []
            for u in range(A_UNROLL):
                s = ss[u] * (SCALE * LOG2E) + b_s[jnp.minimum(idx[u][2], 1), :, 0:nk]
                m = jnp.max(s, axis=1, keepdims=True)
                p = jnp.exp2(s - m)
                ls.append(jnp.sum(p, axis=1, keepdims=True))
                ms.append(m)
                ps.append(p.astype(BF16))
            for u in range(A_UNROLL):
                t, r, i = idx[u]
                o = jnp.dot(ps[u], vs[u], preferred_element_type=F32) / ls[u]
                lse = jnp.broadcast_to(ms[u] + jnp.log2(ls[u]), (A_BLK, LANES))
                if d == 1:
                    rows = pl.ds(pl.multiple_of(t * A_BLK, A_BLK), A_BLK)
                    o_s[gi, rows, :] = o
                    l_s[gi, rows, :] = lse
                elif d == PS:
                    rows = pl.ds(i * (A_BLK * d) + r, A_BLK, stride=d)
                    o_s[gi, rows, :] = o
                    l_s[gi, rows, :] = lse
                else:
                    rows = pl.ds(r // PS, A_BLK, stride=PS)
                    t_s[0, r % PS, rows, :] = o
                    t_s[1, r % PS, rows, :] = lse
            return carry

        lax.fori_loop(0, nblk // A_UNROLL, blk, 0)
        if d == PS * PS:
            for r0 in range(PS):
                def second(cidx, carry, r0=r0, gi=gi):
                    src = pl.ds(pl.multiple_of(cidx * A_BLK, A_BLK), A_BLK)
                    dst = pl.ds(cidx * (A_BLK * PS) + r0, A_BLK, stride=PS)
                    o_s[gi, dst, :] = t_s[0, r0, src, :]
                    l_s[gi, dst, :] = t_s[1, r0, src, :]
                    return carry
                lax.fori_loop(0, S // PS // A_BLK, second, 0)

    def merge(c, carry):
        rows = pl.ds(pl.multiple_of(c * A_BLK, A_BLK), A_BLK)
        l1, l2, l3 = l_s[0, rows, :], l_s[1, rows, :], l_s[2, rows, :]
        mx = jnp.maximum(jnp.maximum(l1, l2), l3)
        e1, e2, e3 = jnp.exp2(l1 - mx), jnp.exp2(l2 - mx), jnp.exp2(l3 - mx)
        num = e1 * o_s[0, rows, :] + e2 * o_s[1, rows, :] + e3 * o_s[2, rows, :]
        z = z_ref[0, 0, 0, rows, :]
        y_ref[0, 0, rows, :] = ((num * z) / ((e1 + e2 + e3) * (1.0 + jnp.exp2(z * -LOG2E)))).astype(BF16)
        return carry

    lax.fori_loop(0, nblk, merge, 0, unroll=2)


def _attn_a(qkv, z):
    _, B, nh, S, _ = z.shape
    in_specs, args = [], []
    for arr, (_, d) in zip(qkv, A_GROUPS):
        L = S // d
        for t in range(3):
            in_specs.append(pl.BlockSpec((d, 1, 1, L, LANES), lambda b, h, t=t, nh=nh: (0, b, t * nh + h, 0, 0)))
            args.append(arr)
    in_specs.append(pl.BlockSpec((1, 1, 1, S, LANES), lambda b, h: (0, b, h, 0, 0)))
    args.append(z)
    return pl.pallas_call(
        functools.partial(_attn_a_kernel, S=S),
        grid=(B, nh),
        in_specs=in_specs,
        out_specs=pl.BlockSpec((1, 1, S, LANES), lambda b, h: (b, h, 0, 0)),
        out_shape=jax.ShapeDtypeStruct((B, nh, S, LANES), BF16),
        scratch_shapes=[pltpu.VMEM((3, S, LANES), F32), pltpu.VMEM((3, S, LANES), F32),
                        pltpu.VMEM((2, 4, S // 4, LANES), F32), pltpu.VMEM((2, A_BLK, 2 * A_BLK), F32)],
        compiler_params=_cparams("parallel", "parallel"),
        name="attn_a",
    )(*args)


def _mixer_a(x, gain, w_in, w_out, tables, final_gain=None):
    B, S, D = x.shape
    w_all, j = w_in
    W = lambda c0, c1: w_all[j, :, c0:c1]
    n_g = len(A_GROUPS)
    aw = D
    nh = aw // LANES
    qkv = []
    for g, (_, d) in enumerate(A_GROUPS):
        c0 = g * 3 * aw
        wg = [_rope_layout(W(c0, c0 + 2 * aw)), W(c0 + 2 * aw, c0 + 3 * aw)]
        parts = [(3 * nh, BF16, 2 * nh)]
        if d == 1:
            wg.append(W(n_g * 3 * aw, (n_g * 3 + 1) * aw))
            parts.append((nh, F32, 0))
        outs = _norm_proj(x, gain, jnp.concatenate(wg, axis=1).astype(BF16), parts, d=d, tables=tables)
        qkv.append(outs[0])
        if d == 1:
            z = outs[1]
    y = _attn_a(qkv, z)
    return _out_proj(y, w_out.astype(BF16), x, final_gain)


def _compress_kernel(k_ref, v_ref, pek_ref, pev_ref, w1k_ref, w1v_ref, w2k_ref, w2v_ref, o_ref):
    def one(c_ref, pe_ref, w1_ref, w2_ref):
        half = c_ref.shape[-1]
        ch = c_ref[0, 0]
        u = jnp.dot((ch + pe_ref[0:1, :]).astype(BF16), w1_ref[0:half, :], preferred_element_type=F32)
        v = jnp.dot((ch + pe_ref[1:2, :]).astype(BF16), w1_ref[half:2 * half, :], preferred_element_type=F32)
        h = _silu(u + pltpu.roll(v, v.shape[0] - 1, 0))
        return jnp.dot(h.astype(BF16), w2_ref[...], preferred_element_type=F32).astype(BF16)

    o_ref[0, 0, 0] = one(k_ref, pek_ref, w1k_ref, w2k_ref)
    o_ref[0, 1, 0] = one(v_ref, pev_ref, w1v_ref, w2v_ref)


def _compress(chunks, pe_k, w1_k, w2_k, pe_v, w1_v, w2_v):
    B, _, nch, cw = chunks.shape
    pe = lambda p: p.reshape(2, cw)
    w1 = lambda w: w.reshape(CMP_LEN * HEAD_DIM, -1).astype(BF16)
    const = lambda shape: pl.BlockSpec(shape, lambda b, g: (0,) * len(shape))
    return pl.pallas_call(
        _compress_kernel,
        grid=(B, B_KV),
        in_specs=[
            pl.BlockSpec((1, 1, nch, cw), lambda b, g: (b, g, 0, 0)),
            pl.BlockSpec((1, 1, nch, cw), lambda b, g: (b, B_KV + g, 0, 0)),
            const((2, cw)), const((2, cw)),
            const((CMP_LEN * HEAD_DIM, HEAD_DIM)), const((CMP_LEN * HEAD_DIM, HEAD_DIM)),
            const((HEAD_DIM, HEAD_DIM)), const((HEAD_DIM, HEAD_DIM)),
        ],
        out_specs=pl.BlockSpec((1, 2, 1, nch, HEAD_DIM), lambda b, g: (b, 0, g, 0, 0)),
        out_shape=jax.ShapeDtypeStruct((B, 2, B_KV, nch, HEAD_DIM), BF16),
        compiler_params=_cparams("parallel", "parallel"),
        name="compress",
    )(chunks, chunks, pe(pe_k), pe(pe_v), w1(w1_k), w1(w1_v), w2_k.astype(BF16), w2_v.astype(BF16))


def _halves(x):
    return x[:, 0:LANES], x[:, LANES:2 * LANES]


def _attn_b_kernel(q_ref, ksel_ref, vsel_ref, kwin_ref, vwin_ref, kc_ref, vc_ref, z_ref, gt_ref, gb_ref,
                   cov_ref, oh_ref, y_ref, kaug_ref, s_ref, mp_ref, lp_ref, acc_ref, *, S):
    TQ = B_TQ
    R = B_REP * TQ
    n_slc = S // SEL_LEN
    ncmp = S // CMP_STRIDE
    qi = pl.program_id(2)
    t0 = qi * TQ

    @pl.when(qi == 0)
    def _():
        kaug_ref[:, 0:HEAD_DIM] = ksel_ref[0, 0, 0]
        kaug_ref[:, HEAD_DIM:2 * HEAD_DIM] = oh_ref[...]

    q = q_ref[0, 0].reshape(R, HEAD_DIM)
    kc = kc_ref[0, 0, 0]
    vc = vc_ref[0, 0, 0]
    dn_t = (((1,), (1,)), ((), ()))

    s = lax.dot_general(q, kc, dn_t, preferred_element_type=F32) * SCALE
    tq = t0 + lax.broadcasted_iota(jnp.int32, (R, ncmp), 0) % TQ
    blk_end = lax.broadcasted_iota(jnp.int32, (R, ncmp), 1) * CMP_STRIDE + (CMP_LEN - 1)
    cmask = blk_end <= tq
    sm = jnp.where(cmask, s, NEG)
    m = jnp.max(sm, axis=1, keepdims=True)
    e = jnp.where(cmask, jnp.exp(sm - m), 0.0)
    den = jnp.sum(e, axis=1, keepdims=True)
    o_cmp = jnp.dot(e.astype(BF16), vc, preferred_element_type=F32) / jnp.maximum(den, 1e-30)

    st = lax.dot_general(kc, q, dn_t, preferred_element_type=F32) * SCALE
    tq_t = t0 + lax.broadcasted_iota(jnp.int32, (ncmp, R), 1) % TQ
    blk_end_t = lax.broadcasted_iota(jnp.int32, (ncmp, R), 0) * CMP_STRIDE + (CMP_LEN - 1)
    cmask_t = blk_end_t <= tq_t
    smt = jnp.where(cmask_t, st, NEG)
    mt = jnp.max(smt, axis=0, keepdims=True)
    et = jnp.where(cmask_t, jnp.exp(smt - mt), 0.0)
    pt = et / jnp.maximum(jnp.sum(et, axis=0, keepdims=True), 1e-30)
    psum = pt[:, 0:TQ]
    for r in range(1, B_REP):
        psum = psum + pt[:, r * TQ:(r + 1) * TQ]
    p_hi = psum.astype(BF16)
    p_lo = (psum - p_hi.astype(F32)).astype(BF16)
    cov = cov_ref[...]
    p_slc = (jnp.dot(cov, p_hi, preferred_element_type=F32)
             + jnp.dot(cov, p_lo, preferred_element_type=F32))
    nblk = lax.broadcasted_iota(jnp.int32, (n_slc, TQ), 0)
    cur = (t0 + lax.broadcasted_iota(jnp.int32, (n_slc, TQ), 1)) // SEL_LEN
    forced = (nblk == 0) | (nblk == cur) | (nblk == cur - 1)
    allowed = nblk <= cur
    score = jnp.where(allowed, jnp.where(forced, FORCE_SCORE, p_slc), -jnp.inf)
    rank = jnp.zeros((n_slc, TQ), F32)
    for mblk in range(n_slc):
        row = score[mblk:mblk + 1, :]
        beats = (row > score) | ((row == score) & (nblk > mblk))
        rank = rank + jnp.where(beats, 1.0, 0.0)
    sel_t = jnp.where((rank < N_SELECT) & allowed, 1.0, 0.0)
    sel_pad = jnp.concatenate([sel_t, jnp.zeros((LANES - n_slc, TQ), F32)], axis=0)
    sel = sel_pad.T
    lane = lax.broadcasted_iota(jnp.int32, (TQ, LANES), 1)
    bias = jnp.where((lane < n_slc) & (sel < 0.5), NEG, 0.0).astype(BF16)
    qaug = jnp.concatenate([q, jnp.concatenate([bias] * B_REP, axis=0)], axis=1)

    row_t = lax.broadcasted_iota(jnp.int32, (R, TQ), 0) % TQ
    col_t = lax.broadcasted_iota(jnp.int32, (R, TQ), 1)
    rows_d = pl.ds(pl.multiple_of(t0, TQ), TQ)

    def sel_scores(c, nch):
        rows = pl.ds(pl.multiple_of(c * TQ, TQ), nch * TQ)
        s = lax.dot_general(qaug, kaug_ref[rows, :], dn_t, preferred_element_type=F32)
        mx = mp_ref[...]
        for u in range(nch):
            su = s[:, u * TQ:(u + 1) * TQ]
            s_ref[c + u] = su
            s0, s1 = _halves(su)
            mx = jnp.maximum(mx, jnp.maximum(s0, s1))
        mp_ref[...] = mx

    def sel_scores_pair(c2, carry):
        sel_scores(2 * c2, 2)
        return carry

    mp_ref[...] = jnp.full((R, LANES), NEG, F32)
    lax.fori_loop(0, qi // 2, sel_scores_pair, 0)
    pl.when(qi % 2 == 1)(lambda: sel_scores(qi - 1, 1))
    s = lax.dot_general(qaug, kaug_ref[rows_d, :], dn_t, preferred_element_type=F32)
    s = jnp.where(col_t <= row_t, s, NEG)
    s_ref[qi] = s
    s0, s1 = _halves(s)
    m = jnp.max(jnp.maximum(mp_ref[...], jnp.maximum(s0, s1)), axis=1, keepdims=True)
    mp_ref[...] = jnp.broadcast_to(m, (R, LANES))
    lp_ref[...] = jnp.zeros((R, LANES), F32)
    acc_ref[...] = jnp.zeros((R, HEAD_DIM), F32)

    def sel_values(c, nch):
        rows = pl.ds(pl.multiple_of(c * TQ, TQ), nch * TQ)
        mb = mp_ref[...]
        lsum = lp_ref[...]
        ps = []
        for u in range(nch):
            s0, s1 = _halves(s_ref[c + u])
            p0 = jnp.exp2((s0 - mb) * (SCALE * LOG2E))
            p1 = jnp.exp2((s1 - mb) * (SCALE * LOG2E))
            lsum = lsum + (p0 + p1)
            ps += [p0.astype(BF16), p1.astype(BF16)]
        lp_ref[...] = lsum
        acc_ref[...] += jnp.dot(jnp.concatenate(ps, axis=1), vsel_ref[0, 0, 0, rows, :],
                                preferred_element_type=F32)

    def sel_values_pair(c2, carry):
        sel_values(2 * c2, 2)
        return carry

    lax.fori_loop(0, (qi + 1) // 2, sel_values_pair, 0)
    pl.when(qi % 2 == 0)(lambda: sel_values(qi, 1))
    o_sel = acc_ref[...] / jnp.sum(lp_ref[...], axis=1, keepdims=True)

    nprev = -(-(WIN - 1) // TQ)
    ss, vs = [], []
    for j in range(nprev + 1):
        cc = qi - nprev + j
        rows = pl.ds(pl.multiple_of(jnp.maximum(cc, 0) * TQ, TQ), TQ)
        s = lax.dot_general(q, kwin_ref[0, 0, 0, rows, :], dn_t, preferred_element_type=F32)
        dist = (nprev - j) * TQ + row_t - col_t
        if j == nprev:
            s = jnp.where(dist >= 0, s, NEG)
        else:
            if (nprev - j) * TQ + TQ - 1 > WIN - 1:
                s = jnp.where(dist <= WIN - 1, s, NEG)
            s = s + jnp.where(cc >= 0, 0.0, NEG)
        ss.append(s)
        vs.append(vwin_ref[0, 0, 0, rows, :])
    mw = ss[0]
    for s in ss[1:]:
        mw = jnp.maximum(mw, s)
    mw0, mw1 = _halves(mw)
    m = jnp.max(jnp.maximum(mw0, mw1), axis=1, keepdims=True)
    mb = jnp.broadcast_to(m, (R, LANES))
    lw = jnp.zeros((R, LANES), F32)
    o_win = jnp.zeros((R, HEAD_DIM), F32)
    for s, v in zip(ss, vs):
        s0, s1 = _halves(s)
        p0 = jnp.exp2((s0 - mb) * (SCALE * LOG2E))
        p1 = jnp.exp2((s1 - mb) * (SCALE * LOG2E))
        lw = lw + (p0 + p1)
        p = jnp.concatenate([p0.astype(BF16), p1.astype(BF16)], axis=1)
        o_win = o_win + jnp.dot(p, v, preferred_element_type=F32)
    o_win = o_win / jnp.sum(lw, axis=1, keepdims=True)

    gates = _sigmoid(gt_ref[0, 0, 0] + gb_ref[0])
    for r in range(B_REP):
        rs = slice(r * TQ, (r + 1) * TQ)
        o = (gates[:, 3 * r:3 * r + 1] * o_cmp[rs]
             + gates[:, 3 * r + 1:3 * r + 2] * o_sel[rs]
             + gates[:, 3 * r + 2:3 * r + 3] * o_win[rs])
        y_ref[0, r] = (o * _silu(z_ref[0, 0, r])).astype(BF16)


def _attn_b(hb, hf, kvc, gate_b):
    _, B, _, S, _ = hb.shape
    TQ = B_TQ
    n_slc = S // SEL_LEN
    ncmp = S // CMP_STRIDE
    j = np.arange(ncmp)[None, :]
    n = np.arange(n_slc)[:, None]
    cover_t = ((j * CMP_STRIDE < (n + 1) * SEL_LEN) & (j * CMP_STRIDE + CMP_LEN > n * SEL_LEN)
               & (j < ncmp - 1)).astype(np.float32)
    onehot = (np.arange(S)[:, None] // SEL_LEN == np.arange(LANES)[None, :]).astype(np.float32)
    nq = B_REP * B_KV
    kv = lambda s0: pl.BlockSpec((1, 1, 1, S, LANES), lambda b, g, i, s0=s0: (0, b, s0 + g, 0, 0))
    return pl.pallas_call(
        functools.partial(_attn_b_kernel, S=S),
        grid=(B, B_KV, S // TQ),
        in_specs=[
            pl.BlockSpec((1, 1, B_REP, TQ, LANES), lambda b, g, i: (0, b, g, i, 0)),
            kv(nq), kv(nq + 2 * B_KV), kv(nq + B_KV), kv(nq + 3 * B_KV),
            pl.BlockSpec((1, 1, 1, ncmp, LANES), lambda b, g, i: (b, 0, g, 0, 0)),
            pl.BlockSpec((1, 1, 1, ncmp, LANES), lambda b, g, i: (b, 1, g, 0, 0)),
            pl.BlockSpec((1, 1, B_REP, TQ, LANES), lambda b, g, i: (0, b, g, i, 0)),
            pl.BlockSpec((1, 1, 1, TQ, LANES), lambda b, g, i: (0, b, nq + g, i, 0)),
            pl.BlockSpec((1, 1, LANES), lambda b, g, i: (g, 0, 0)),
            pl.BlockSpec((n_slc, ncmp), lambda b, g, i: (0, 0)),
            pl.BlockSpec((S, LANES), lambda b, g, i: (0, 0)),
        ],
        out_specs=pl.BlockSpec((1, B_REP, TQ, LANES), lambda b, g, i: (b, g, i, 0)),
        out_shape=jax.ShapeDtypeStruct((B, nq, S, LANES), BF16),
        scratch_shapes=[pltpu.VMEM((S, 2 * HEAD_DIM), BF16),
                        pltpu.VMEM((S // TQ, B_REP * TQ, TQ), F32),
                        pltpu.VMEM((B_REP * TQ, LANES), F32),
                        pltpu.VMEM((B_REP * TQ, LANES), F32),
                        pltpu.VMEM((B_REP * TQ, HEAD_DIM), F32)],
        compiler_params=_cparams("parallel", "parallel", "arbitrary"),
        name="attn_b",
    )(hb, hb, hb, hb, hb, kvc, kvc, hf, hf, gate_b, jnp.asarray(cover_t, BF16), jnp.asarray(onehot, BF16))


def _mixer_b(x, gain, w_in, gate_b, pe_k, w1_k, w2_k, pe_v, w1_v, w2_v, w_out, tables, final_gain=None):
    B, S, D = x.shape
    bw = D
    kvw = B_KV * HEAD_DIM
    o0, o1, o2 = bw, bw + 6 * kvw, 2 * bw + 6 * kvw
    W = w_in
    kvcol = lambda c: W[:, o0 + c * kvw:o0 + (c + 1) * kvw]
    wg = W[:, o2:].reshape(D, B_KV, B_REP * 3)
    wg = jnp.pad(wg, ((0, 0), (0, 0), (0, LANES - B_REP * 3))).reshape(D, B_KV * LANES)
    gb = jnp.pad(gate_b.reshape(B_KV, 1, B_REP * 3), ((0, 0), (0, 0), (0, LANES - B_REP * 3)))
    w_b = [_rope_layout(W[:, :o0]), _rope_layout(kvcol(2)), _rope_layout(kvcol(4)), kvcol(3), kvcol(5)]
    w_f = [_rope_layout(kvcol(0)), kvcol(1), W[:, o1:o2], wg]
    parts = [((bw + 4 * kvw) // LANES, BF16, (bw + 2 * kvw) // LANES),
             (2 * kvw // LANES, F32, kvw // LANES, CMP_STRIDE),
             (bw // LANES + B_KV, F32, 0)]
    hb, hc, hf = _norm_proj(x, gain, jnp.concatenate(w_b + w_f, axis=1).astype(BF16), parts, d=1, tables=tables)
    w1_kl = jnp.swapaxes(_rope_layout(jnp.swapaxes(w1_k, 1, 2)), 1, 2)
    kvc = _compress(hc[0], _rope_layout(pe_k), w1_kl, _rope_layout(w2_k), pe_v, w1_v, w2_v)
    y = _attn_b(hb, hf, kvc, gb)
    return _out_proj(y, w_out.astype(BF16), x, final_gain)


C_PAD = 8


def _layer_c_kernel(*refs, final):
    if final:
        (x_ref, g_ref, wi_ref, cw_ref, cb_ref, wa_ref, ba_ref, wx_ref, bx_ref, lam_ref, wo_ref, fg_ref,
         o_ref, xn_ref, xpad_ref, z_ref, y_ref, h_ref) = refs
    else:
        (x_ref, g_ref, wi_ref, cw_ref, cb_ref, wa_ref, ba_ref, wx_ref, bx_ref, lam_ref, wo_ref,
         o_ref, xn_ref, xpad_ref, z_ref, y_ref, h_ref) = refs
    tc = x_ref.shape[1]
    cwid = z_ref.shape[1]

    @pl.when(pl.program_id(1) == 0)
    def _():
        xpad_ref[0:C_PAD, :] = jnp.zeros((C_PAD, cwid), F32)
        h_ref[...] = jnp.zeros((1, cwid), F32)

    xf = x_ref[0]
    ms = jnp.mean(xf * xf, axis=-1, keepdims=True)
    xn_ref[...] = (xf * lax.rsqrt(ms + NORM_EPS) * g_ref[...]).astype(BF16)

    def project(dst, col0, lo, hi):
        c = lo
        while c < hi:
            tn = min(PROJ_TN, hi - c)
            dst(c, tn, jnp.dot(xn_ref[...], wi_ref[:, col0 + c:col0 + c + tn], preferred_element_type=F32))
            c += tn

    def put_x(c, tn, v):
        xpad_ref[C_PAD:C_PAD + tc, c:c + tn] = v

    def put_z(c, tn, v):
        z_ref[:, c:c + tn] = v

    project(put_x, 0, 0, cwid)
    project(put_z, cwid, 0, cwid)

    row = lax.broadcasted_iota(jnp.int32, (C_CHUNK, LANES), 0)
    for n in range(cwid // LANES):
        cols = slice(n * LANES, (n + 1) * LANES)
        nlam = -lam_ref[:, cols]
        softplus = jnp.maximum(nlam, 0.0) + jnp.log(1.0 + jnp.exp(-jnp.abs(nlam)))
        h_prev = h_ref[:, cols]
        for c in range(tc // C_CHUNK):
            base = c * C_CHUNK
            xc = jnp.broadcast_to(cb_ref[:, cols], (C_CHUNK, LANES))
            for k in range(CONV_W):
                r0 = base + C_PAD - (CONV_W - 1) + k
                xc = xc + cw_ref[k:k + 1, cols] * xpad_ref[r0:r0 + C_CHUNK, cols]
            xcb = xc.astype(BF16)
            r = _sigmoid(jnp.dot(xcb, wa_ref[n], preferred_element_type=F32) + ba_ref[:, cols])
            i = _sigmoid(jnp.dot(xcb, wx_ref[n], preferred_element_type=F32) + bx_ref[:, cols])
            a = jnp.exp2(r * (softplus * (-LRU_C * LOG2E)))
            b = jnp.sqrt(1.0 - a * a) * i * xc
            k = 1
            while k < C_GROUP:
                keep = row % C_GROUP >= k
                a_s = jnp.where(keep, pltpu.roll(a, k, 0), 1.0)
                b_s = jnp.where(keep, pltpu.roll(b, k, 0), 0.0)
                b = a * b_s + b
                a = a * a_s
                k *= 2
            hs = []
            for g in range(C_CHUNK // C_GROUP):
                rows = slice(g * C_GROUP, (g + 1) * C_GROUP)
                hs.append(b[rows] + a[rows] * h_prev)
                h_prev = hs[-1][C_GROUP - 1:C_GROUP, :]
            h = jnp.concatenate(hs, axis=0)
            y_ref[base:base + C_CHUNK, cols] = (h * _silu(z_ref[base:base + C_CHUNK, cols])).astype(BF16)
        h_ref[:, cols] = h_prev

    xpad_ref[0:C_PAD, :] = xpad_ref[tc:tc + C_PAD, :]
    xo = xf + jnp.dot(y_ref[...], wo_ref[...], preferred_element_type=F32)
    if final:
        ms = jnp.mean(xo * xo, axis=-1, keepdims=True)
        xo = xo * lax.rsqrt(ms + NORM_EPS) * fg_ref[...]
    o_ref[0] = xo


def _mixer_c(x, gain, w_in, conv_w, conv_b, wa, ba, wx, bx, lam, w_out, final_gain=None, tc=512):
    B, S, D = x.shape
    cwid = w_out.shape[0]
    final = final_gain is not None
    const = lambda shape: pl.BlockSpec(shape, lambda b, t: (0,) * len(shape))
    vec = lambda v: v.reshape(1, -1)
    in_specs = [
        pl.BlockSpec((1, tc, D), lambda b, t: (b, t, 0)),
        const((1, D)), const((D, 2 * cwid)), const((CONV_W, cwid)), const((1, cwid)),
        const(wa.shape), const((1, cwid)), const(wx.shape), const((1, cwid)), const((1, cwid)),
        const((cwid, D)),
    ]
    args = [x, vec(gain), w_in.astype(BF16), conv_w, vec(conv_b), wa.astype(BF16), vec(ba), wx.astype(BF16),
            vec(bx), vec(lam), w_out.astype(BF16)]
    if final:
        in_specs.append(const((1, D)))
        args.append(vec(final_gain))
    return pl.pallas_call(
        functools.partial(_layer_c_kernel, final=final),
        grid=(B, S // tc),
        in_specs=in_specs,
        out_specs=pl.BlockSpec((1, tc, D), lambda b, t: (b, t, 0)),
        out_shape=jax.ShapeDtypeStruct((B, S, D), F32),
        scratch_shapes=[pltpu.VMEM((tc, D), BF16), pltpu.VMEM((C_PAD + tc, cwid), F32),
                        pltpu.VMEM((tc, cwid), F32), pltpu.VMEM((tc, cwid), BF16), pltpu.VMEM((1, cwid), F32)],
        compiler_params=_cparams("parallel", "arbitrary"),
        name="layer_c",
    )(*args)


def kernel(x, norm_g, final_g, a_w_in, a_w_out, b_w_in, b_gate_b, b_pe_k, b_w1_k, b_w2_k, b_pe_v, b_w1_v,
           b_w2_v, b_w_out, c_w_in, c_conv_w, c_conv_b, c_wa, c_ba, c_wx, c_bx, c_lambda, c_w_out):
    depth = norm_g.shape[0]
    tables = _rope_tables(x.shape[1])
    for i in range(depth):
        kind, j = i % 3, i // 3
        final = final_g if i == depth - 1 else None
        if kind == 0:
            x = _mixer_a(x, norm_g[i], (a_w_in, j), a_w_out[j], tables, final)
        elif kind == 1:
            x = _mixer_b(x, norm_g[i], b_w_in[j], b_gate_b[j], b_pe_k[j], b_w1_k[j], b_w2_k[j],
                         b_pe_v[j], b_w1_v[j], b_w2_v[j], b_w_out[j], tables, final)
        else:
            x = _mixer_c(x, norm_g[i], c_w_in[j], c_conv_w[j], c_conv_b[j], c_wa[j], c_ba[j],
                         c_wx[j], c_bx[j], c_lambda[j], c_w_out[j], final)
    return x
```

```python
import functools

import numpy as np
import jax
import jax.numpy as jnp
from jax import lax
from jax.experimental import pallas as pl
from jax.experimental.pallas import tpu as pltpu

F32 = jnp.float32
BF16 = jnp.bfloat16

HEAD_DIM = 128
LANES = 128
SUBLANES = 8
ROPE_DIM = HEAD_DIM // 4
ROPE_HALF = ROPE_DIM // 2
ROPE_THETA = 500000.0
NORM_EPS = 1e-6
SCALE = HEAD_DIM ** -0.5
NEG = -1e30
VMEM_LIMIT = 48 * 1024 * 1024

A_GROUPS = ((128, 1), (512, 4), (2048, 16))
A_BLK = 128
A_UNROLL = 16
B_KV = 2
B_REP = 4
CMP_LEN = 32
CMP_STRIDE = 16
SEL_LEN = 64
N_SELECT = 16
WIN = 512
FORCE_SCORE = 1000.0
B_TQ = 256
C_BLOCKS = 10
CONV_W = 4
LRU_C = 8.0
C_CHUNK = 256
C_GROUP = 32


def _cparams(*sem):
    return pltpu.CompilerParams(dimension_semantics=sem, vmem_limit_bytes=VMEM_LIMIT)


LOG2E = 1.4426950408889634


def _sigmoid(x):
    return 1.0 / (1.0 + jnp.exp2(x * -LOG2E))


def _silu(x):
    return x * _sigmoid(x)


ROPE_SWAP = LANES // 2


def _rope_layout(w):
    lead = w.shape[:-1]
    h = w.reshape(*lead, -1, HEAD_DIM)
    h = jnp.concatenate([h[..., :ROPE_HALF], h[..., ROPE_DIM:ROPE_SWAP + ROPE_HALF],
                         h[..., ROPE_HALF:ROPE_DIM], h[..., ROPE_SWAP + ROPE_HALF:]], axis=-1)
    return h.reshape(*lead, -1)


def _rope_tables(S):
    inv_freq = ROPE_THETA ** (-2.0 * np.arange(ROPE_HALF, dtype=np.float64) / ROPE_DIM)
    ang = np.arange(S, dtype=np.float64)[:, None] * inv_freq[None, :]
    cos, sin = np.cos(ang), np.sin(ang)
    gap = ROPE_SWAP - ROPE_HALF
    c = np.concatenate([cos, np.ones((S, gap)), cos, np.ones((S, gap))], axis=1).astype(np.float32)
    sn = np.concatenate([-sin, np.zeros((S, gap)), sin, np.zeros((S, gap))], axis=1).astype(np.float32)
    return c, sn


PROJ_TN = 512


def _dilated_row_copies(x_hbm, xbuf, sem, step, slot, *, d, split, nb, lt, n_mt):
    r = step // n_mt
    mi = step % n_mt
    ls = lt // split
    return [pltpu.make_async_copy(x_hbm.at[mi * nb + bb, :, r + d * j, :],
                                  xbuf.at[slot, pl.ds(bb * lt + j * ls, ls), :], sem.at[slot])
            for bb in range(nb) for j in range(split)]


def _norm_proj_kernel(*refs, nb, lt, parts, any_rope, gather):
    n_out = len(parts)
    split = 1
    if any(p[3] for p in parts):
        cstage = refs[-1]
        refs = refs[:-1]
    if gather:
        d, split, n_mt = gather
        if split > 1:
            stage = refs[-1]
            refs = refs[:-1]
        xbuf, sem = refs[-2:]
        refs = refs[:-2]
        step = pl.program_id(0) * n_mt + pl.program_id(1)
        slot = step % 2
        copies = functools.partial(_dilated_row_copies, refs[0], xbuf, sem, d=d, split=split, nb=nb, lt=lt,
                                   n_mt=n_mt)

        @pl.when(step == 0)
        def _():
            for cp in copies(0, 0):
                cp.start()

        @pl.when(step + 1 < d * n_mt)
        def _():
            for cp in copies(step + 1, 1 - slot):
                cp.start()

        for cp in copies(step, slot):
            cp.wait()
    o_refs, xn_ref = refs[-n_out - 1:-1], refs[-1]
    if any_rope:
        x_ref, g_ref, w_ref, c_ref, sn_ref = refs[:5]
    else:
        x_ref, g_ref, w_ref = refs[:3]
    xf = xbuf[slot] if gather else x_ref[...]
    ms = jnp.mean(xf * xf, axis=-1, keepdims=True)
    xn_ref[...] = (xf * lax.rsqrt(ms + NORM_EPS) * g_ref[...]).astype(BF16)
    col = 0
    for o_ref, (ns, _, n_rope, ck) in zip(o_refs, parts):
        hs = PROJ_TN // LANES if ns % (PROJ_TN // LANES) == 0 else 2
        assert ns % hs == 0
        for c in range(ns // hs):
            tn = hs * LANES
            res = jnp.dot(xn_ref[...], w_ref[:, col:col + tn], preferred_element_type=F32)
            col += tn
            for bb in range(nb):
                for hh in range(hs):
                    sub = res[bb * lt:(bb + 1) * lt, hh * LANES:(hh + 1) * LANES]
                    if c * hs + hh < n_rope:
                        sub = sub * c_ref[...] + pltpu.roll(sub, ROPE_SWAP, 1) * sn_ref[...]
                    if split > 1:
                        ls = lt // split
                        sb = (c * hs + hh) % 2
                        for j in range(split):
                            stage[sb, pl.ds(j, ls, stride=split), :] = sub[j * ls:(j + 1) * ls]
                        sub = stage[sb]
                    if ck:
                        cstage[...] = sub
                        for p in range(ck):
                            o_ref[0, bb, c * hs + hh, :, p * LANES:(p + 1) * LANES] = (
                                cstage[pl.ds(p, lt // ck, stride=ck), :].astype(o_ref.dtype))
                    else:
                        o_ref[0, bb, c * hs + hh] = sub.astype(o_ref.dtype)


def _norm_proj(x, gain, w, parts, *, d, tables, tm=512):
    B, S, D = x.shape
    N = w.shape[1]
    L = S // d
    rows = B * L
    tm = min(tm, rows)
    assert N == LANES * sum(p[0] for p in parts) and rows % tm == 0
    assert (tm % L == 0) or (L % tm == 0)
    parts = [tuple(p) + (0,) * (4 - len(p)) for p in parts]
    if tm >= L:
        nb, lt = tm // L, L
        o_spec = lambda ns, ck: pl.BlockSpec((1, nb, ns, L // ck, ck * LANES), lambda r, mi: (r, mi, 0, 0, 0))
        t_map = lambda r, mi: (0, r)
    else:
        nb, lt, tps = 1, tm, L // tm
        o_spec = lambda ns, ck: pl.BlockSpec((1, 1, ns, tm // ck, ck * LANES),
                                             lambda r, mi: (r, mi // tps, 0, mi % tps, 0))
        t_map = lambda r, mi: (mi % tps, r)
    split = max(1, SUBLANES // d)
    gather = (d, split, rows // tm) if d > 1 else None
    if gather:
        assert tm >= L and (d * split) % SUBLANES == 0
        x_spec, x_arg = pl.BlockSpec(memory_space=pl.ANY), x.reshape(B, L // split, d * split, D)
        scratch = [pltpu.VMEM((2, tm, D), F32), pltpu.SemaphoreType.DMA((2,))]
        if split > 1:
            scratch.append(pltpu.VMEM((2, lt, LANES), F32))
        sem = ("arbitrary", "arbitrary")
    else:
        x_spec, x_arg = pl.BlockSpec((tm, D), lambda r, mi: (mi, r)), x.reshape(rows, D)
        scratch = []
        sem = ("parallel", "parallel")
    in_specs = [
        x_spec,
        pl.BlockSpec((1, D), lambda r, mi: (0, 0)),
        pl.BlockSpec((D, N), lambda r, mi: (0, 0)),
    ]
    args = [x_arg, gain.reshape(1, D), w]
    any_rope = any(p[2] for p in parts)
    if any_rope:
        in_specs += [pl.BlockSpec((lt, LANES), t_map)] * 2
        for t in tables:
            t = t.reshape(L, d * LANES)
            if gather and split > 1:
                t = np.concatenate([t[j::split] for j in range(split)], axis=0)
            args.append(t)
    if any(p[3] for p in parts):
        scratch.append(pltpu.VMEM((lt, LANES), F32))
    return pl.pallas_call(
        functools.partial(_norm_proj_kernel, nb=nb, lt=lt, parts=tuple(parts), any_rope=any_rope, gather=gather),
        grid=(d, rows // tm),
        in_specs=in_specs,
        out_specs=[o_spec(p[0], max(p[3], 1)) for p in parts],
        out_shape=[jax.ShapeDtypeStruct((d, B, p[0], L // max(p[3], 1), max(p[3], 1) * LANES), p[1])
                   for p in parts],
        scratch_shapes=[pltpu.VMEM((tm, D), BF16)] + scratch,
        compiler_params=_cparams(*sem),
        name=f"norm_proj_d{d}_n{N}",
    )(*args)


def _out_proj_kernel(*refs, nh, final):
    if final:
        y_ref, w_ref, x_ref, g_ref, o_ref = refs
    else:
        y_ref, w_ref, x_ref, o_ref = refs
    y = jnp.concatenate([y_ref[0, h] for h in range(nh)], axis=1)
    xn = x_ref[0] + jnp.dot(y, w_ref[...], preferred_element_type=F32)
    if final:
        ms = jnp.mean(xn * xn, axis=-1, keepdims=True)
        xn = xn * lax.rsqrt(ms + NORM_EPS) * g_ref[...]
    o_ref[0] = xn


def _out_proj(y, w, x, final_gain=None, tm=512):
    B, nh, S, _ = y.shape
    D = x.shape[-1]
    final = final_gain is not None
    in_specs = [
        pl.BlockSpec((1, nh, tm, LANES), lambda b, i: (b, 0, i, 0)),
        pl.BlockSpec((nh * LANES, D), lambda b, i: (0, 0)),
        pl.BlockSpec((1, tm, D), lambda b, i: (b, i, 0)),
    ]
    args = [y, w, x]
    if final:
        in_specs.append(pl.BlockSpec((1, D), lambda b, i: (0, 0)))
        args.append(final_gain.reshape(1, D))
    return pl.pallas_call(
        functools.partial(_out_proj_kernel, nh=nh, final=final),
        grid=(B, S // tm),
        in_specs=in_specs,
        out_specs=pl.BlockSpec((1, tm, D), lambda b, i: (b, i, 0)),
        out_shape=jax.ShapeDtypeStruct((B, S, D), F32),
        compiler_params=_cparams("parallel", "parallel"),
        name="out_proj",
    )(*args)


def _attn_a_kernel(q1, k1, v1, q2, k2, v2, q3, k3, v3, z_ref, y_ref, o_s, l_s, t_s, b_s, *, S):
    groups = ((q1, k1, v1), (q2, k2, v2), (q3, k3, v3))
    nblk = S // A_BLK
    PS = 4

    a = lax.broadcasted_iota(jnp.int32, (A_BLK, 2 * A_BLK), 0)
    c = lax.broadcasted_iota(jnp.int32, (A_BLK, 2 * A_BLK), 1)
    b_s[0] = jnp.where(c <= a, 0.0, NEG)
    b_s[1] = jnp.where((c >= a) & (c <= a + A_BLK), 0.0, NEG)

    for gi, ((q_ref, k_ref, v_ref), (win, d)) in enumerate(zip(groups, A_GROUPS)):
        L = S // d
        nqb = L // A_BLK
        nk = 2 * A_BLK if nqb > 1 else A_BLK
        assert win // d == A_BLK and d in (1, PS, PS * PS)

        def blk(tt, carry, q_ref=q_ref, k_ref=k_ref, v_ref=v_ref, d=d, nqb=nqb, nk=nk, gi=gi):
            idx, ss, vs = [], [], []
            for u in range(A_UNROLL):
                t = tt * A_UNROLL + u
                r = t // nqb
                i = t % nqb
                ks = jnp.maximum(i - 1, 0) * A_BLK
                q = q_ref[r, 0, 0, pl.ds(pl.multiple_of(i * A_BLK, A_BLK), A_BLK), :]
                k = k_ref[r, 0, 0, pl.ds(pl.multiple_of(ks, A_BLK), nk), :]
                vs.append(v_ref[r, 0, 0, pl.ds(pl.multiple_of(ks, A_BLK), nk), :])
                ss.append(lax.dot_general(q, k, (((1,), (1,)), ((), ())), preferred_element_type=F32))
                idx.append((t, r, i))
            ps, ms, ls = [], [], []
            for u in range(A_UNROLL):
                s = ss[u] * (SCALE * LOG2E) + b_s[jnp.minimum(idx[u][2], 1), :, 0:nk]
                m = jnp.max(s, axis=1, keepdims=True)
                p = jnp.exp2(s - m)
                ls.append(jnp.sum(p, axis=1, keepdims=True))
                ms.append(m)
                ps.append(p.astype(BF16))
            for u in range(A_UNROLL):
                t, r, i = idx[u]
                o = jnp.dot(ps[u], vs[u], preferred_element_type=F32) / ls[u]
                lse = jnp.broadcast_to(ms[u] + jnp.log2(ls[u]), (A_BLK, LANES))
                if d == 1:
                    rows = pl.ds(pl.multiple_of(t * A_BLK, A_BLK), A_BLK)
                    o_s[gi, rows, :] = o
                    l_s[gi, rows, :] = lse
                elif d == PS:
                    rows = pl.ds(i * (A_BLK * d) + r, A_BLK, stride=d)
                    o_s[gi, rows, :] = o
                    l_s[gi, rows, :] = lse
                else:
                    rows = pl.ds(r // PS, A_BLK, stride=PS)
                    t_s[0, r % PS, rows, :] = o
                    t_s[1, r % PS, rows, :] = lse
            return carry

        lax.fori_loop(0, nblk // A_UNROLL, blk, 0)
        if d == PS * PS:
            for r0 in range(PS):
                def second(cidx, carry, r0=r0, gi=gi):
                    src = pl.ds(pl.multiple_of(cidx * A_BLK, A_BLK), A_BLK)
                    dst = pl.ds(cidx * (A_BLK * PS) + r0, A_BLK, stride=PS)
                    o_s[gi, dst, :] = t_s[0, r0, src, :]
                    l_s[gi, dst, :] = t_s[1, r0, src, :]
                    return carry
                lax.fori_loop(0, S // PS // A_BLK, second, 0)

    def merge(c, carry):
        rows = pl.ds(pl.multiple_of(c * A_BLK, A_BLK), A_BLK)
        l1, l2, l3 = l_s[0, rows, :], l_s[1, rows, :], l_s[2, rows, :]
        mx = jnp.maximum(jnp.maximum(l1, l2), l3)
        e1, e2, e3 = jnp.exp2(l1 - mx), jnp.exp2(l2 - mx), jnp.exp2(l3 - mx)
        num = e1 * o_s[0, rows, :] + e2 * o_s[1, rows, :] + e3 * o_s[2, rows, :]
        z = z_ref[0, 0, 0, rows, :]
        y_ref[0, 0, rows, :] = ((num * z) / ((e1 + e2 + e3) * (1.0 + jnp.exp2(z * -LOG2E)))).astype(BF16)
        return carry

    lax.fori_loop(0, nblk, merge, 0, unroll=2)


def _attn_a(qkv, z):
    _, B, nh, S, _ = z.shape
    in_specs, args = [], []
    for arr, (_, d) in zip(qkv, A_GROUPS):
        L = S // d
        for t in range(3):
            in_specs.append(pl.BlockSpec((d, 1, 1, L, LANES), lambda b, h, t=t, nh=nh: (0, b, t * nh + h, 0, 0)))
            args.append(arr)
    in_specs.append(pl.BlockSpec((1, 1, 1, S, LANES), lambda b, h: (0, b, h, 0, 0)))
    args.append(z)
    return pl.pallas_call(
        functools.partial(_attn_a_kernel, S=S),
        grid=(B, nh),
        in_specs=in_specs,
        out_specs=pl.BlockSpec((1, 1, S, LANES), lambda b, h: (b, h, 0, 0)),
        out_shape=jax.ShapeDtypeStruct((B, nh, S, LANES), BF16),
        scratch_shapes=[pltpu.VMEM((3, S, LANES), F32), pltpu.VMEM((3, S, LANES), F32),
                        pltpu.VMEM((2, 4, S // 4, LANES), F32), pltpu.VMEM((2, A_BLK, 2 * A_BLK), F32)],
        compiler_params=_cparams("parallel", "parallel"),
        name="attn_a",
    )(*args)


def _mixer_a(x, gain, w_in, w_out, tables, final_gain=None):
    B, S, D = x.shape
    w_all, j = w_in
    W = lambda c0, c1: w_all[j, :, c0:c1]
    n_g = len(A_GROUPS)
    aw = D
    nh = aw // LANES
    qkv = []
    for g, (_, d) in enumerate(A_GROUPS):
        c0 = g * 3 * aw
        wg = [_rope_layout(W(c0, c0 + 2 * aw)), W(c0 + 2 * aw, c0 + 3 * aw)]
        parts = [(3 * nh, BF16, 2 * nh)]
        if d == 1:
            wg.append(W(n_g * 3 * aw, (n_g * 3 + 1) * aw))
            parts.append((nh, F32, 0))
        outs = _norm_proj(x, gain, jnp.concatenate(wg, axis=1).astype(BF16), parts, d=d, tables=tables)
        qkv.append(outs[0])
        if d == 1:
            z = outs[1]
    y = _attn_a(qkv, z)
    return _out_proj(y, w_out.astype(BF16), x, final_gain)


def _compress_kernel(k_ref, v_ref, pek_ref, pev_ref, w1k_ref, w1v_ref, w2k_ref, w2v_ref, o_ref):
    def one(c_ref, pe_ref, w1_ref, w2_ref):
        half = c_ref.shape[-1]
        ch = c_ref[0, 0]
        u = jnp.dot((ch + pe_ref[0:1, :]).astype(BF16), w1_ref[0:half, :], preferred_element_type=F32)
        v = jnp.dot((ch + pe_ref[1:2, :]).astype(BF16), w1_ref[half:2 * half, :], preferred_element_type=F32)
        h = _silu(u + pltpu.roll(v, v.shape[0] - 1, 0))
        return jnp.dot(h.astype(BF16), w2_ref[...], preferred_element_type=F32).astype(BF16)

    o_ref[0, 0, 0] = one(k_ref, pek_ref, w1k_ref, w2k_ref)
    o_ref[0, 1, 0] = one(v_ref, pev_ref, w1v_ref, w2v_ref)


def _compress(chunks, pe_k, w1_k, w2_k, pe_v, w1_v, w2_v):
    B, _, nch, cw = chunks.shape
    pe = lambda p: p.reshape(2, cw)
    w1 = lambda w: w.reshape(CMP_LEN * HEAD_DIM, -1).astype(BF16)
    const = lambda shape: pl.BlockSpec(shape, lambda b, g: (0,) * len(shape))
    return pl.pallas_call(
        _compress_kernel,
        grid=(B, B_KV),
        in_specs=[
            pl.BlockSpec((1, 1, nch, cw), lambda b, g: (b, g, 0, 0)),
            pl.BlockSpec((1, 1, nch, cw), lambda b, g: (b, B_KV + g, 0, 0)),
            const((2, cw)), const((2, cw)),
            const((CMP_LEN * HEAD_DIM, HEAD_DIM)), const((CMP_LEN * HEAD_DIM, HEAD_DIM)),
            const((HEAD_DIM, HEAD_DIM)), const((HEAD_DIM, HEAD_DIM)),
        ],
        out_specs=pl.BlockSpec((1, 2, 1, nch, HEAD_DIM), lambda b, g: (b, 0, g, 0, 0)),
        out_shape=jax.ShapeDtypeStruct((B, 2, B_KV, nch, HEAD_DIM), BF16),
        compiler_params=_cparams("parallel", "parallel"),
        name="compress",
    )(chunks, chunks, pe(pe_k), pe(pe_v), w1(w1_k), w1(w1_v), w2_k.astype(BF16), w2_v.astype(BF16))


def _halves(x):
    return x[:, 0:LANES], x[:, LANES:2 * LANES]


def _attn_b_kernel(q_ref, ksel_ref, vsel_ref, kwin_ref, vwin_ref, kc_ref, vc_ref, z_ref, gt_ref, gb_ref,
                   cov_ref, oh_ref, y_ref, kaug_ref, s_ref, mp_ref, lp_ref, acc_ref, w_ref, wm_ref, *, S):
    TQ = B_TQ
    R = B_REP * TQ
    n_slc = S // SEL_LEN
    ncmp = S // CMP_STRIDE
    qi = pl.program_id(2)
    t0 = qi * TQ

    @pl.when(qi == 0)
    def _():
        kaug_ref[:, 0:HEAD_DIM] = ksel_ref[0, 0, 0]
        kaug_ref[:, HEAD_DIM:2 * HEAD_DIM] = oh_ref[...]

    q = q_ref[0, 0].reshape(R, HEAD_DIM)
    kc = kc_ref[0, 0, 0]
    vc = vc_ref[0, 0, 0]
    dn_t = (((1,), (1,)), ((), ()))

    row_t = lax.broadcasted_iota(jnp.int32, (R, TQ), 0) % TQ
    col_t = lax.broadcasted_iota(jnp.int32, (R, TQ), 1)

    nprev = -(-(WIN - 1) // TQ)
    mw = None
    for j in range(nprev + 1):
        cc = qi - nprev + j
        rows = pl.ds(pl.multiple_of(jnp.maximum(cc, 0) * TQ, TQ), TQ)
        s = lax.dot_general(q, kwin_ref[0, 0, 0, rows, :], dn_t, preferred_element_type=F32)
        dist = (nprev - j) * TQ + row_t - col_t
        if j == nprev:
            s = jnp.where(dist >= 0, s, NEG)
        else:
            if (nprev - j) * TQ + TQ - 1 > WIN - 1:
                s = jnp.where(dist <= WIN - 1, s, NEG)
            s = s + jnp.where(cc >= 0, 0.0, NEG)
        w_ref[j] = s
        s0, s1 = _halves(s)
        mw = jnp.maximum(s0, s1) if mw is None else jnp.maximum(mw, jnp.maximum(s0, s1))
    wm_ref[...] = mw

    s = lax.dot_general(q, kc, dn_t, preferred_element_type=F32) * SCALE
    tq = t0 + lax.broadcasted_iota(jnp.int32, (R, ncmp), 0) % TQ
    blk_end = lax.broadcasted_iota(jnp.int32, (R, ncmp), 1) * CMP_STRIDE + (CMP_LEN - 1)
    cmask = blk_end <= tq
    sm = jnp.where(cmask, s, NEG)
    m = jnp.max(sm, axis=1, keepdims=True)
    e = jnp.where(cmask, jnp.exp(sm - m), 0.0)
    den = jnp.sum(e, axis=1, keepdims=True)
    p = e * (1.0 / jnp.maximum(den, 1e-30))
    o_cmp = jnp.dot(p.astype(BF16), vc, preferred_element_type=F32)

    psum = p[0:TQ]
    for r in range(1, B_REP):
        psum = psum + p[r * TQ:(r + 1) * TQ]
    p_hi = psum.astype(BF16)
    p_lo = (psum - p_hi.astype(F32)).astype(BF16)
    cov = cov_ref[...]
    p_slc = (jnp.dot(p_hi, cov, preferred_element_type=F32)
             + jnp.dot(p_lo, cov, preferred_element_type=F32)).T[0:n_slc]
    nblk = lax.broadcasted_iota(jnp.int32, (n_slc, TQ), 0)
    cur = (t0 + lax.broadcasted_iota(jnp.int32, (n_slc, TQ), 1)) // SEL_LEN
    forced = (nblk == 0) | (nblk == cur) | (nblk == cur - 1)
    allowed = nblk <= cur
    score = jnp.where(allowed, jnp.where(forced, FORCE_SCORE, p_slc), -jnp.inf)
    rank = jnp.zeros((n_slc, TQ), F32)
    for mblk in range(n_slc):
        row = score[mblk:mblk + 1, :]
        beats = (row > score) | ((row == score) & (nblk > mblk))
        rank = rank + jnp.where(beats, 1.0, 0.0)
    sel_t = jnp.where((rank < N_SELECT) & allowed, 1.0, 0.0)
    sel_pad = jnp.concatenate([sel_t, jnp.zeros((LANES - n_slc, TQ), F32)], axis=0)
    sel = sel_pad.T
    lane = lax.broadcasted_iota(jnp.int32, (TQ, LANES), 1)
    bias = jnp.where((lane < n_slc) & (sel < 0.5), NEG, 0.0).astype(BF16)
    qaug = jnp.concatenate([q, jnp.concatenate([bias] * B_REP, axis=0)], axis=1)

    rows_d = pl.ds(pl.multiple_of(t0, TQ), TQ)

    def sel_scores(c, nch):
        rows = pl.ds(pl.multiple_of(c * TQ, TQ), nch * TQ)
        s = lax.dot_general(qaug, kaug_ref[rows, :], dn_t, preferred_element_type=F32)
        mx = mp_ref[...]
        for u in range(nch):
            su = s[:, u * TQ:(u + 1) * TQ]
            s_ref[c + u] = su
            s0, s1 = _halves(su)
            mx = jnp.maximum(mx, jnp.maximum(s0, s1))
        mp_ref[...] = mx

    def sel_scores_pair(c2, carry):
        sel_scores(2 * c2, 2)
        return carry

    mp_ref[...] = jnp.full((R, LANES), NEG, F32)
    lax.fori_loop(0, qi // 2, sel_scores_pair, 0)
    pl.when(qi % 2 == 1)(lambda: sel_scores(qi - 1, 1))
    s = lax.dot_general(qaug, kaug_ref[rows_d, :], dn_t, preferred_element_type=F32)
    s = jnp.where(col_t <= row_t, s, NEG)
    s_ref[qi] = s
    s0, s1 = _halves(s)
    m = jnp.max(jnp.maximum(mp_ref[...], jnp.maximum(s0, s1)), axis=1, keepdims=True)
    mp_ref[...] = jnp.broadcast_to(m, (R, LANES))
    lp_ref[...] = jnp.zeros((R, LANES), F32)
    acc_ref[...] = jnp.zeros((R, HEAD_DIM), F32)

    def sel_values(c, nch):
        rows = pl.ds(pl.multiple_of(c * TQ, TQ), nch * TQ)
        mb = mp_ref[...]
        lsum = lp_ref[...]
        ps = []
        for u in range(nch):
            s0, s1 = _halves(s_ref[c + u])
            p0 = jnp.exp2((s0 - mb) * (SCALE * LOG2E))
            p1 = jnp.exp2((s1 - mb) * (SCALE * LOG2E))
            lsum = lsum + (p0 + p1)
            ps += [p0.astype(BF16), p1.astype(BF16)]
        lp_ref[...] = lsum
        acc_ref[...] += jnp.dot(jnp.concatenate(ps, axis=1), vsel_ref[0, 0, 0, rows, :],
                                preferred_element_type=F32)

    def sel_values_pair(c2, carry):
        sel_values(2 * c2, 2)
        return carry

    lax.fori_loop(0, (qi + 1) // 2, sel_values_pair, 0)
    pl.when(qi % 2 == 0)(lambda: sel_values(qi, 1))
    o_sel = acc_ref[...] / jnp.sum(lp_ref[...], axis=1, keepdims=True)

    m = jnp.max(wm_ref[...], axis=1, keepdims=True)
    mb = jnp.broadcast_to(m, (R, LANES))
    lw = jnp.zeros((R, LANES), F32)
    o_win = jnp.zeros((R, HEAD_DIM), F32)
    for j in range(nprev + 1):
        rows = pl.ds(pl.multiple_of(jnp.maximum(qi - nprev + j, 0) * TQ, TQ), TQ)
        v = vwin_ref[0, 0, 0, rows, :]
        s0, s1 = _halves(w_ref[j])
        p0 = jnp.exp2((s0 - mb) * (SCALE * LOG2E))
        p1 = jnp.exp2((s1 - mb) * (SCALE * LOG2E))
        lw = lw + (p0 + p1)
        p = jnp.concatenate([p0.astype(BF16), p1.astype(BF16)], axis=1)
        o_win = o_win + jnp.dot(p, v, preferred_element_type=F32)
    o_win = o_win / jnp.sum(lw, axis=1, keepdims=True)

    gates = _sigmoid(gt_ref[0, 0, 0] + gb_ref[0])
    for r in range(B_REP):
        rs = slice(r * TQ, (r + 1) * TQ)
        o = (gates[:, 3 * r:3 * r + 1] * o_cmp[rs]
             + gates[:, 3 * r + 1:3 * r + 2] * o_sel[rs]
             + gates[:, 3 * r + 2:3 * r + 3] * o_win[rs])
        y_ref[0, r] = (o * _silu(z_ref[0, 0, r])).astype(BF16)


def _attn_b(hb, hf, kvc, gate_b):
    _, B, _, S, _ = hb.shape
    TQ = B_TQ
    n_slc = S // SEL_LEN
    ncmp = S // CMP_STRIDE
    j = np.arange(ncmp)[:, None]
    n = np.arange(LANES)[None, :]
    cover = ((j * CMP_STRIDE < (n + 1) * SEL_LEN) & (j * CMP_STRIDE + CMP_LEN > n * SEL_LEN)
             & (j < ncmp - 1) & (n < n_slc)).astype(np.float32)
    onehot = (np.arange(S)[:, None] // SEL_LEN == np.arange(LANES)[None, :]).astype(np.float32)
    nq = B_REP * B_KV
    kv = lambda s0: pl.BlockSpec((1, 1, 1, S, LANES), lambda b, g, i, s0=s0: (0, b, s0 + g, 0, 0))
    return pl.pallas_call(
        functools.partial(_attn_b_kernel, S=S),
        grid=(B, B_KV, S // TQ),
        in_specs=[
            pl.BlockSpec((1, 1, B_REP, TQ, LANES), lambda b, g, i: (0, b, g, i, 0)),
            kv(nq), kv(nq + 2 * B_KV), kv(nq + B_KV), kv(nq + 3 * B_KV),
            pl.BlockSpec((1, 1, 1, ncmp, LANES), lambda b, g, i: (b, 0, g, 0, 0)),
            pl.BlockSpec((1, 1, 1, ncmp, LANES), lambda b, g, i: (b, 1, g, 0, 0)),
            pl.BlockSpec((1, 1, B_REP, TQ, LANES), lambda b, g, i: (0, b, g, i, 0)),
            pl.BlockSpec((1, 1, 1, TQ, LANES), lambda b, g, i: (0, b, nq + g, i, 0)),
            pl.BlockSpec((1, 1, LANES), lambda b, g, i: (g, 0, 0)),
            pl.BlockSpec((ncmp, LANES), lambda b, g, i: (0, 0)),
            pl.BlockSpec((S, LANES), lambda b, g, i: (0, 0)),
        ],
        out_specs=pl.BlockSpec((1, B_REP, TQ, LANES), lambda b, g, i: (b, g, i, 0)),
        out_shape=jax.ShapeDtypeStruct((B, nq, S, LANES), BF16),
        scratch_shapes=[pltpu.VMEM((S, 2 * HEAD_DIM), BF16),
                        pltpu.VMEM((S // TQ, B_REP * TQ, TQ), F32),
                        pltpu.VMEM((B_REP * TQ, LANES), F32),
                        pltpu.VMEM((B_REP * TQ, LANES), F32),
                        pltpu.VMEM((B_REP * TQ, HEAD_DIM), F32),
                        pltpu.VMEM((-(-(WIN - 1) // TQ) + 1, B_REP * TQ, TQ), F32),
                        pltpu.VMEM((B_REP * TQ, LANES), F32)],
        compiler_params=_cparams("parallel", "parallel", "arbitrary"),
        name="attn_b",
    )(hb, hb, hb, hb, hb, kvc, kvc, hf, hf, gate_b, jnp.asarray(cover, BF16), jnp.asarray(onehot, BF16))


def _mixer_b(x, gain, w_in, gate_b, pe_k, w1_k, w2_k, pe_v, w1_v, w2_v, w_out, tables, final_gain=None):
    B, S, D = x.shape
    bw = D
    kvw = B_KV * HEAD_DIM
    o0, o1, o2 = bw, bw + 6 * kvw, 2 * bw + 6 * kvw
    W = w_in
    kvcol = lambda c: W[:, o0 + c * kvw:o0 + (c + 1) * kvw]
    wg = W[:, o2:].reshape(D, B_KV, B_REP * 3)
    wg = jnp.pad(wg, ((0, 0), (0, 0), (0, LANES - B_REP * 3))).reshape(D, B_KV * LANES)
    gb = jnp.pad(gate_b.reshape(B_KV, 1, B_REP * 3), ((0, 0), (0, 0), (0, LANES - B_REP * 3)))
    w_b = [_rope_layout(W[:, :o0]), _rope_layout(kvcol(2)), _rope_layout(kvcol(4)), kvcol(3), kvcol(5)]
    w_f = [_rope_layout(kvcol(0)), kvcol(1), W[:, o1:o2], wg]
    parts = [((bw + 4 * kvw) // LANES, BF16, (bw + 2 * kvw) // LANES),
             (2 * kvw // LANES, F32, kvw // LANES, CMP_STRIDE),
             (bw // LANES + B_KV, F32, 0)]
    hb, hc, hf = _norm_proj(x, gain, jnp.concatenate(w_b + w_f, axis=1).astype(BF16), parts, d=1, tables=tables)
    w1_kl = jnp.swapaxes(_rope_layout(jnp.swapaxes(w1_k, 1, 2)), 1, 2)
    kvc = _compress(hc[0], _rope_layout(pe_k), w1_kl, _rope_layout(w2_k), pe_v, w1_v, w2_v)
    y = _attn_b(hb, hf, kvc, gb)
    return _out_proj(y, w_out.astype(BF16), x, final_gain)


C_PAD = 8


def _layer_c_kernel(*refs, final):
    if final:
        (x_ref, g_ref, wi_ref, cw_ref, cb_ref, wa_ref, ba_ref, wx_ref, bx_ref, lam_ref, wo_ref, fg_ref,
         o_ref, xn_ref, xpad_ref, z_ref, y_ref, h_ref) = refs
    else:
        (x_ref, g_ref, wi_ref, cw_ref, cb_ref, wa_ref, ba_ref, wx_ref, bx_ref, lam_ref, wo_ref,
         o_ref, xn_ref, xpad_ref, z_ref, y_ref, h_ref) = refs
    tc = x_ref.shape[1]
    cwid = z_ref.shape[1]

    @pl.when(pl.program_id(1) == 0)
    def _():
        xpad_ref[0:C_PAD, :] = jnp.zeros((C_PAD, cwid), F32)
        h_ref[...] = jnp.zeros((1, cwid), F32)

    xf = x_ref[0]
    ms = jnp.mean(xf * xf, axis=-1, keepdims=True)
    xn_ref[...] = (xf * lax.rsqrt(ms + NORM_EPS) * g_ref[...]).astype(BF16)

    def project(dst, col0, lo, hi):
        c = lo
        while c < hi:
            tn = min(PROJ_TN, hi - c)
            dst(c, tn, jnp.dot(xn_ref[...], wi_ref[:, col0 + c:col0 + c + tn], preferred_element_type=F32))
            c += tn

    def put_x(c, tn, v):
        xpad_ref[C_PAD:C_PAD + tc, c:c + tn] = v

    def put_z(c, tn, v):
        z_ref[:, c:c + tn] = v

    project(put_x, 0, 0, cwid)
    project(put_z, cwid, 0, cwid)

    row = lax.broadcasted_iota(jnp.int32, (C_CHUNK, LANES), 0)
    for n in range(cwid // LANES):
        cols = slice(n * LANES, (n + 1) * LANES)
        nlam = -lam_ref[:, cols]
        softplus = jnp.maximum(nlam, 0.0) + jnp.log(1.0 + jnp.exp(-jnp.abs(nlam)))
        h_prev = h_ref[:, cols]
        for c in range(tc // C_CHUNK):
            base = c * C_CHUNK
            xc = jnp.broadcast_to(cb_ref[:, cols], (C_CHUNK, LANES))
            for k in range(CONV_W):
                r0 = base + C_PAD - (CONV_W - 1) + k
                xc = xc + cw_ref[k:k + 1, cols] * xpad_ref[r0:r0 + C_CHUNK, cols]
            xcb = xc.astype(BF16)
            r = _sigmoid(jnp.dot(xcb, wa_ref[n], preferred_element_type=F32) + ba_ref[:, cols])
            i = _sigmoid(jnp.dot(xcb, wx_ref[n], preferred_element_type=F32) + bx_ref[:, cols])
            a = jnp.exp2(r * (softplus * (-LRU_C * LOG2E)))
            b = jnp.sqrt(1.0 - a * a) * i * xc
            k = 1
            while k < C_GROUP:
                keep = row % C_GROUP >= k
                a_s = jnp.where(keep, pltpu.roll(a, k, 0), 1.0)
                b_s = jnp.where(keep, pltpu.roll(b, k, 0), 0.0)
                b = a * b_s + b
                a = a * a_s
                k *= 2
            hs = []
            for g in range(C_CHUNK // C_GROUP):
                rows = slice(g * C_GROUP, (g + 1) * C_GROUP)
                hs.append(b[rows] + a[rows] * h_prev)
                h_prev = hs[-1][C_GROUP - 1:C_GROUP, :]
            h = jnp.concatenate(hs, axis=0)
            y_ref[base:base + C_CHUNK, cols] = (h * _silu(z_ref[base:base + C_CHUNK, cols])).astype(BF16)
        h_ref[:, cols] = h_prev

    xpad_ref[0:C_PAD, :] = xpad_ref[tc:tc + C_PAD, :]
    xo = xf + jnp.dot(y_ref[...], wo_ref[...], preferred_element_type=F32)
    if final:
        ms = jnp.mean(xo * xo, axis=-1, keepdims=True)
        xo = xo * lax.rsqrt(ms + NORM_EPS) * fg_ref[...]
    o_ref[0] = xo


def _mixer_c(x, gain, w_in, conv_w, conv_b, wa, ba, wx, bx, lam, w_out, final_gain=None, tc=512):
    B, S, D = x.shape
    cwid = w_out.shape[0]
    final = final_gain is not None
    const = lambda shape: pl.BlockSpec(shape, lambda b, t: (0,) * len(shape))
    vec = lambda v: v.reshape(1, -1)
    in_specs = [
        pl.BlockSpec((1, tc, D), lambda b, t: (b, t, 0)),
        const((1, D)), const((D, 2 * cwid)), const((CONV_W, cwid)), const((1, cwid)),
        const(wa.shape), const((1, cwid)), const(wx.shape), const((1, cwid)), const((1, cwid)),
        const((cwid, D)),
    ]
    args = [x, vec(gain), w_in.astype(BF16), conv_w, vec(conv_b), wa.astype(BF16), vec(ba), wx.astype(BF16),
            vec(bx), vec(lam), w_out.astype(BF16)]
    if final:
        in_specs.append(const((1, D)))
        args.append(vec(final_gain))
    return pl.pallas_call(
        functools.partial(_layer_c_kernel, final=final),
        grid=(B, S // tc),
        in_specs=in_specs,
        out_specs=pl.BlockSpec((1, tc, D), lambda b, t: (b, t, 0)),
        out_shape=jax.ShapeDtypeStruct((B, S, D), F32),
        scratch_shapes=[pltpu.VMEM((tc, D), BF16), pltpu.VMEM((C_PAD + tc, cwid), F32),
                        pltpu.VMEM((tc, cwid), F32), pltpu.VMEM((tc, cwid), BF16), pltpu.VMEM((1, cwid), F32)],
        compiler_params=_cparams("parallel", "arbitrary"),
        name="layer_c",
    )(*args)


def kernel(x, norm_g, final_g, a_w_in, a_w_out, b_w_in, b_gate_b, b_pe_k, b_w1_k, b_w2_k, b_pe_v, b_w1_v,
           b_w2_v, b_w_out, c_w_in, c_conv_w, c_conv_b, c_wa, c_ba, c_wx, c_bx, c_lambda, c_w_out):
    depth = norm_g.shape[0]
    tables = _rope_tables(x.shape[1])
    for i in range(depth):
        kind, j = i % 3, i // 3
        final = final_g if i == depth - 1 else None
        if kind == 0:
            x = _mixer_a(x, norm_g[i], (a_w_in, j), a_w_out[j], tables, final)
        elif kind == 1:
            x = _mixer_b(x, norm_g[i], b_w_in[j], b_gate_b[j], b_pe_k[j], b_w1_k[j], b_w2_k[j],
                         b_pe_v[j], b_w1_v[j], b_w2_v[j], b_w_out[j], tables, final)
        else:
            x = _mixer_c(x, norm_g[i], c_w_in[j], c_conv_w[j], c_conv_b[j], c_wa[j], c_ba[j],
                         c_wx[j], c_bx[j], c_lambda[j], c_w_out[j], final)
    return x
```

```python
import functools

import numpy as np
import jax
import jax.numpy as jnp
from jax import lax
from jax.experimental import pallas as pl
from jax.experimental.pallas import tpu as pltpu

F32 = jnp.float32
BF16 = jnp.bfloat16

HEAD_DIM = 128
LANES = 128
SUBLANES = 8
ROPE_DIM = HEAD_DIM // 4
ROPE_HALF = ROPE_DIM // 2
ROPE_THETA = 500000.0
NORM_EPS = 1e-6
SCALE = HEAD_DIM ** -0.5
NEG = -1e30
VMEM_LIMIT = 56 * 1024 * 1024

A_GROUPS = ((128, 1), (512, 4), (2048, 16))
A_BLK = 128
A_UNROLL = 16
B_KV = 2
B_REP = 4
CMP_LEN = 32
CMP_STRIDE = 16
SEL_LEN = 64
N_SELECT = 16
WIN = 512
FORCE_SCORE = 1000.0
B_TQ = 256
C_BLOCKS = 10
CONV_W = 4
LRU_C = 8.0
C_CHUNK = 256
C_GROUP = 32


def _cparams(*sem):
    return pltpu.CompilerParams(dimension_semantics=sem, vmem_limit_bytes=VMEM_LIMIT)


LOG2E = 1.4426950408889634


def _sigmoid(x):
    return 1.0 / (1.0 + jnp.exp2(x * -LOG2E))


def _silu(x):
    return x * _sigmoid(x)


ROPE_SWAP = LANES // 2


def _rope_layout(w):
    lead = w.shape[:-1]
    h = w.reshape(*lead, -1, HEAD_DIM)
    h = jnp.concatenate([h[..., :ROPE_HALF], h[..., ROPE_DIM:ROPE_SWAP + ROPE_HALF],
                         h[..., ROPE_HALF:ROPE_DIM], h[..., ROPE_SWAP + ROPE_HALF:]], axis=-1)
    return h.reshape(*lead, -1)


def _rope_tables(S):
    inv_freq = ROPE_THETA ** (-2.0 * np.arange(ROPE_HALF, dtype=np.float64) / ROPE_DIM)
    ang = np.arange(S, dtype=np.float64)[:, None] * inv_freq[None, :]
    cos, sin = np.cos(ang), np.sin(ang)
    gap = ROPE_SWAP - ROPE_HALF
    c = np.concatenate([cos, np.ones((S, gap)), cos, np.ones((S, gap))], axis=1).astype(np.float32)
    sn = np.concatenate([-sin, np.zeros((S, gap)), sin, np.zeros((S, gap))], axis=1).astype(np.float32)
    return c, sn


PROJ_TN = 512


def _dilated_row_copies(x_hbm, xbuf, sem, step, slot, *, d, split, nb, lt, n_mt):
    r = step // n_mt
    mi = step % n_mt
    ls = lt // split
    return [pltpu.make_async_copy(x_hbm.at[mi * nb + bb, :, r + d * j, :],
                                  xbuf.at[slot, pl.ds(bb * lt + j * ls, ls), :], sem.at[slot])
            for bb in range(nb) for j in range(split)]


def _norm_proj_kernel(*refs, nb, lt, parts, any_rope, gather):
    n_out = len(parts)
    split = 1
    if any(p[3] for p in parts):
        cstage = refs[-1]
        refs = refs[:-1]
    if gather:
        d, split, n_mt = gather
        if split > 1:
            stage = refs[-1]
            refs = refs[:-1]
        xbuf, sem = refs[-2:]
        refs = refs[:-2]
        step = pl.program_id(0) * n_mt + pl.program_id(1)
        slot = step % 2
        copies = functools.partial(_dilated_row_copies, refs[0], xbuf, sem, d=d, split=split, nb=nb, lt=lt,
                                   n_mt=n_mt)

        @pl.when(step == 0)
        def _():
            for cp in copies(0, 0):
                cp.start()

        @pl.when(step + 1 < d * n_mt)
        def _():
            for cp in copies(step + 1, 1 - slot):
                cp.start()

        for cp in copies(step, slot):
            cp.wait()
    o_refs, xn_ref = refs[-n_out - 1:-1], refs[-1]
    if any_rope:
        x_ref, g_ref, w_ref, c_ref, sn_ref = refs[:5]
    else:
        x_ref, g_ref, w_ref = refs[:3]
    xf = xbuf[slot] if gather else x_ref[...]
    ms = jnp.mean(xf * xf, axis=-1, keepdims=True)
    xn_ref[...] = (xf * lax.rsqrt(ms + NORM_EPS) * g_ref[...]).astype(BF16)
    col = 0
    for o_ref, (ns, _, n_rope, ck) in zip(o_refs, parts):
        hs = PROJ_TN // LANES if ns % (PROJ_TN // LANES) == 0 else 2
        assert ns % hs == 0
        for c in range(ns // hs):
            tn = hs * LANES
            res = jnp.dot(xn_ref[...], w_ref[:, col:col + tn], preferred_element_type=F32)
            col += tn
            for bb in range(nb):
                for hh in range(hs):
                    sub = res[bb * lt:(bb + 1) * lt, hh * LANES:(hh + 1) * LANES]
                    if c * hs + hh < n_rope:
                        sub = sub * c_ref[...] + pltpu.roll(sub, ROPE_SWAP, 1) * sn_ref[...]
                    if split > 1:
                        ls = lt // split
                        sb = (c * hs + hh) % 2
                        for j in range(split):
                            stage[sb, pl.ds(j, ls, stride=split), :] = sub[j * ls:(j + 1) * ls]
                        sub = stage[sb]
                    if ck:
                        cstage[...] = sub
                        for p in range(ck):
                            o_ref[0, bb, c * hs + hh, :, p * LANES:(p + 1) * LANES] = (
                                cstage[pl.ds(p, lt // ck, stride=ck), :].astype(o_ref.dtype))
                    else:
                        o_ref[0, bb, c * hs + hh] = sub.astype(o_ref.dtype)


def _norm_proj(x, gain, w, parts, *, d, tables, tm=1024):
    B, S, D = x.shape
    N = w.shape[1]
    L = S // d
    rows = B * L
    tm = min(tm, rows)
    assert N == LANES * sum(p[0] for p in parts) and rows % tm == 0
    assert (tm % L == 0) or (L % tm == 0)
    parts = [tuple(p) + (0,) * (4 - len(p)) for p in parts]
    if tm >= L:
        nb, lt = tm // L, L
        o_spec = lambda ns, ck: pl.BlockSpec((1, nb, ns, L // ck, ck * LANES), lambda r, mi: (r, mi, 0, 0, 0))
        t_map = lambda r, mi: (0, r)
    else:
        nb, lt, tps = 1, tm, L // tm
        o_spec = lambda ns, ck: pl.BlockSpec((1, 1, ns, tm // ck, ck * LANES),
                                             lambda r, mi: (r, mi // tps, 0, mi % tps, 0))
        t_map = lambda r, mi: (mi % tps, r)
    split = max(1, SUBLANES // d)
    gather = (d, split, rows // tm) if d > 1 else None
    if gather:
        assert tm >= L and (d * split) % SUBLANES == 0
        x_spec, x_arg = pl.BlockSpec(memory_space=pl.ANY), x.reshape(B, L // split, d * split, D)
        scratch = [pltpu.VMEM((2, tm, D), F32), pltpu.SemaphoreType.DMA((2,))]
        if split > 1:
            scratch.append(pltpu.VMEM((2, lt, LANES), F32))
        sem = ("arbitrary", "arbitrary")
    else:
        x_spec, x_arg = pl.BlockSpec((tm, D), lambda r, mi: (mi, r)), x.reshape(rows, D)
        scratch = []
        sem = ("parallel", "parallel")
    in_specs = [
        x_spec,
        pl.BlockSpec((1, D), lambda r, mi: (0, 0)),
        pl.BlockSpec((D, N), lambda r, mi: (0, 0)),
    ]
    args = [x_arg, gain.reshape(1, D), w]
    any_rope = any(p[2] for p in parts)
    if any_rope:
        in_specs += [pl.BlockSpec((lt, LANES), t_map)] * 2
        for t in tables:
            t = t.reshape(L, d * LANES)
            if gather and split > 1:
                t = np.concatenate([t[j::split] for j in range(split)], axis=0)
            args.append(t)
    if any(p[3] for p in parts):
        scratch.append(pltpu.VMEM((lt, LANES), F32))
    return pl.pallas_call(
        functools.partial(_norm_proj_kernel, nb=nb, lt=lt, parts=tuple(parts), any_rope=any_rope, gather=gather),
        grid=(d, rows // tm),
        in_specs=in_specs,
        out_specs=[o_spec(p[0], max(p[3], 1)) for p in parts],
        out_shape=[jax.ShapeDtypeStruct((d, B, p[0], L // max(p[3], 1), max(p[3], 1) * LANES), p[1])
                   for p in parts],
        scratch_shapes=[pltpu.VMEM((tm, D), BF16)] + scratch,
        compiler_params=_cparams(*sem),
        name=f"norm_proj_d{d}_n{N}",
    )(*args)


def _out_proj_kernel(*refs, nh, final):
    if final:
        y_ref, w_ref, x_ref, g_ref, o_ref = refs
    else:
        y_ref, w_ref, x_ref, o_ref = refs
    y = jnp.concatenate([y_ref[0, h] for h in range(nh)], axis=1)
    xn = x_ref[0] + jnp.dot(y, w_ref[...], preferred_element_type=F32)
    if final:
        ms = jnp.mean(xn * xn, axis=-1, keepdims=True)
        xn = xn * lax.rsqrt(ms + NORM_EPS) * g_ref[...]
    o_ref[0] = xn


def _out_proj(y, w, x, final_gain=None, tm=1024):
    B, nh, S, _ = y.shape
    D = x.shape[-1]
    final = final_gain is not None
    in_specs = [
        pl.BlockSpec((1, nh, tm, LANES), lambda b, i: (b, 0, i, 0)),
        pl.BlockSpec((nh * LANES, D), lambda b, i: (0, 0)),
        pl.BlockSpec((1, tm, D), lambda b, i: (b, i, 0)),
    ]
    args = [y, w, x]
    if final:
        in_specs.append(pl.BlockSpec((1, D), lambda b, i: (0, 0)))
        args.append(final_gain.reshape(1, D))
    return pl.pallas_call(
        functools.partial(_out_proj_kernel, nh=nh, final=final),
        grid=(B, S // tm),
        in_specs=in_specs,
        out_specs=pl.BlockSpec((1, tm, D), lambda b, i: (b, i, 0)),
        out_shape=jax.ShapeDtypeStruct((B, S, D), F32),
        compiler_params=_cparams("parallel", "parallel"),
        name="out_proj",
    )(*args)


def _attn_a_kernel(q1, k1, v1, q2, k2, v2, q3, k3, v3, z_ref, y_ref, o_s, l_s, t_s, b_s, *, S):
    groups = ((q1, k1, v1), (q2, k2, v2), (q3, k3, v3))
    nblk = S // A_BLK
    PS = 4

    a = lax.broadcasted_iota(jnp.int32, (A_BLK, 2 * A_BLK), 0)
    c = lax.broadcasted_iota(jnp.int32, (A_BLK, 2 * A_BLK), 1)
    b_s[0] = jnp.where(c <= a, 0.0, NEG)
    b_s[1] = jnp.where((c >= a) & (c <= a + A_BLK), 0.0, NEG)

    for gi, ((q_ref, k_ref, v_ref), (win, d)) in enumerate(zip(groups, A_GROUPS)):
        L = S // d
        nqb = L // A_BLK
        nk = 2 * A_BLK if nqb > 1 else A_BLK
        assert win // d == A_BLK and d in (1, PS, PS * PS)

        def blk(tt, carry, q_ref=q_ref, k_ref=k_ref, v_ref=v_ref, d=d, nqb=nqb, nk=nk, gi=gi):
            idx, ss, vs = [], [], []
            for u in range(A_UNROLL):
                t = tt * A_UNROLL + u
                r = t // nqb
                i = t % nqb
                ks = jnp.maximum(i - 1, 0) * A_BLK
                q = q_ref[r, 0, 0, pl.ds(pl.multiple_of(i * A_BLK, A_BLK), A_BLK), :]
                k = k_ref[r, 0, 0, pl.ds(pl.multiple_of(ks, A_BLK), nk), :]
                vs.append(v_ref[r, 0, 0, pl.ds(pl.multiple_of(ks, A_BLK), nk), :])
                ss.append(lax.dot_general(q, k, (((1,), (1,)), ((), ())), preferred_element_type=F32))
                idx.append((t, r, i))
            ps, ms, ls = [], [], []
            for u in range(A_UNROLL):
                s = ss[u] * (SCALE * LOG2E) + b_s[jnp.minimum(idx[u][2], 1), :, 0:nk]
                m = jnp.max(s, axis=1, keepdims=True)
                p = jnp.exp2(s - m)
                ls.append(jnp.sum(p, axis=1, keepdims=True))
                ms.append(m)
                ps.append(p.astype(BF16))
            for u in range(A_UNROLL):
                t, r, i = idx[u]
                o = jnp.dot(ps[u], vs[u], preferred_element_type=F32) / ls[u]
                lse = jnp.broadcast_to(ms[u] + jnp.log2(ls[u]), (A_BLK, LANES))
                if d == 1:
                    rows = pl.ds(pl.multiple_of(t * A_BLK, A_BLK), A_BLK)
                    o_s[gi, rows, :] = o
                    l_s[gi, rows, :] = lse
                elif d == PS:
                    rows = pl.ds(i * (A_BLK * d) + r, A_BLK, stride=d)
                    o_s[gi, rows, :] = o
                    l_s[gi, rows, :] = lse
                else:
                    rows = pl.ds(r // PS, A_BLK, stride=PS)
                    t_s[0, r % PS, rows, :] = o
                    t_s[1, r % PS, rows, :] = lse
            return carry

        lax.fori_loop(0, nblk // A_UNROLL, blk, 0)
        if d == PS * PS:
            for r0 in range(PS):
                def second(cidx, carry, r0=r0, gi=gi):
                    src = pl.ds(pl.multiple_of(cidx * A_BLK, A_BLK), A_BLK)
                    dst = pl.ds(cidx * (A_BLK * PS) + r0, A_BLK, stride=PS)
                    o_s[gi, dst, :] = t_s[0, r0, src, :]
                    l_s[gi, dst, :] = t_s[1, r0, src, :]
                    return carry
                lax.fori_loop(0, S // PS // A_BLK, second, 0)

    def merge(c, carry):
        rows = pl.ds(pl.multiple_of(c * A_BLK, A_BLK), A_BLK)
        l1, l2, l3 = l_s[0, rows, :], l_s[1, rows, :], l_s[2, rows, :]
        mx = jnp.maximum(jnp.maximum(l1, l2), l3)
        e1, e2, e3 = jnp.exp2(l1 - mx), jnp.exp2(l2 - mx), jnp.exp2(l3 - mx)
        num = e1 * o_s[0, rows, :] + e2 * o_s[1, rows, :] + e3 * o_s[2, rows, :]
        z = z_ref[0, 0, 0, rows, :]
        y_ref[0, 0, rows, :] = ((num * z) / ((e1 + e2 + e3) * (1.0 + jnp.exp2(z * -LOG2E)))).astype(BF16)
        return carry

    lax.fori_loop(0, nblk, merge, 0, unroll=2)


def _attn_a(qkv, z):
    _, B, nh, S, _ = z.shape
    in_specs, args = [], []
    for arr, (_, d) in zip(qkv, A_GROUPS):
        L = S // d
        for t in range(3):
            in_specs.append(pl.BlockSpec((d, 1, 1, L, LANES), lambda b, h, t=t, nh=nh: (0, b, t * nh + h, 0, 0)))
            args.append(arr)
    in_specs.append(pl.BlockSpec((1, 1, 1, S, LANES), lambda b, h: (0, b, h, 0, 0)))
    args.append(z)
    return pl.pallas_call(
        functools.partial(_attn_a_kernel, S=S),
        grid=(B, nh),
        in_specs=in_specs,
        out_specs=pl.BlockSpec((1, 1, S, LANES), lambda b, h: (b, h, 0, 0)),
        out_shape=jax.ShapeDtypeStruct((B, nh, S, LANES), BF16),
        scratch_shapes=[pltpu.VMEM((3, S, LANES), F32), pltpu.VMEM((3, S, LANES), F32),
                        pltpu.VMEM((2, 4, S // 4, LANES), F32), pltpu.VMEM((2, A_BLK, 2 * A_BLK), F32)],
        compiler_params=_cparams("parallel", "parallel"),
        name="attn_a",
    )(*args)


def _mixer_a(x, gain, w_in, w_out, tables, final_gain=None):
    B, S, D = x.shape
    w_all, j = w_in
    W = lambda c0, c1: w_all[j, :, c0:c1]
    n_g = len(A_GROUPS)
    aw = D
    nh = aw // LANES
    qkv = []
    for g, (_, d) in enumerate(A_GROUPS):
        c0 = g * 3 * aw
        wg = [_rope_layout(W(c0, c0 + 2 * aw)), W(c0 + 2 * aw, c0 + 3 * aw)]
        parts = [(3 * nh, BF16, 2 * nh)]
        if d == 1:
            wg.append(W(n_g * 3 * aw, (n_g * 3 + 1) * aw))
            parts.append((nh, F32, 0))
        outs = _norm_proj(x, gain, jnp.concatenate(wg, axis=1).astype(BF16), parts, d=d, tables=tables)
        qkv.append(outs[0])
        if d == 1:
            z = outs[1]
    y = _attn_a(qkv, z)
    return _out_proj(y, w_out.astype(BF16), x, final_gain)


def _compress_kernel(k_ref, v_ref, pek_ref, pev_ref, w1k_ref, w1v_ref, w2k_ref, w2v_ref, o_ref):
    def one(c_ref, pe_ref, w1_ref, w2_ref):
        half = c_ref.shape[-1]
        ch = c_ref[0, 0]
        u = jnp.dot((ch + pe_ref[0:1, :]).astype(BF16), w1_ref[0:half, :], preferred_element_type=F32)
        v = jnp.dot((ch + pe_ref[1:2, :]).astype(BF16), w1_ref[half:2 * half, :], preferred_element_type=F32)
        h = _silu(u + pltpu.roll(v, v.shape[0] - 1, 0))
        return jnp.dot(h.astype(BF16), w2_ref[...], preferred_element_type=F32).astype(BF16)

    o_ref[0, 0, 0] = one(k_ref, pek_ref, w1k_ref, w2k_ref)
    o_ref[0, 1, 0] = one(v_ref, pev_ref, w1v_ref, w2v_ref)


def _compress(chunks, pe_k, w1_k, w2_k, pe_v, w1_v, w2_v):
    B, _, nch, cw = chunks.shape
    pe = lambda p: p.reshape(2, cw)
    w1 = lambda w: w.reshape(CMP_LEN * HEAD_DIM, -1).astype(BF16)
    const = lambda shape: pl.BlockSpec(shape, lambda b, g: (0,) * len(shape))
    return pl.pallas_call(
        _compress_kernel,
        grid=(B, B_KV),
        in_specs=[
            pl.BlockSpec((1, 1, nch, cw), lambda b, g: (b, g, 0, 0)),
            pl.BlockSpec((1, 1, nch, cw), lambda b, g: (b, B_KV + g, 0, 0)),
            const((2, cw)), const((2, cw)),
            const((CMP_LEN * HEAD_DIM, HEAD_DIM)), const((CMP_LEN * HEAD_DIM, HEAD_DIM)),
            const((HEAD_DIM, HEAD_DIM)), const((HEAD_DIM, HEAD_DIM)),
        ],
        out_specs=pl.BlockSpec((1, 2, 1, nch, HEAD_DIM), lambda b, g: (b, 0, g, 0, 0)),
        out_shape=jax.ShapeDtypeStruct((B, 2, B_KV, nch, HEAD_DIM), BF16),
        compiler_params=_cparams("parallel", "parallel"),
        name="compress",
    )(chunks, chunks, pe(pe_k), pe(pe_v), w1(w1_k), w1(w1_v), w2_k.astype(BF16), w2_v.astype(BF16))


def _halves(x):
    return x[:, 0:LANES], x[:, LANES:2 * LANES]


def _attn_b_kernel(q_ref, ksel_ref, vsel_ref, kwin_ref, vwin_ref, kc_ref, vc_ref, z_ref, gt_ref, gb_ref,
                   cov_ref, oh_ref, y_ref, kaug_ref, s_ref, mp_ref, lp_ref, acc_ref, w_ref, wm_ref, *, S):
    TQ = B_TQ
    R = B_REP * TQ
    n_slc = S // SEL_LEN
    ncmp = S // CMP_STRIDE
    qi = pl.program_id(2)
    t0 = qi * TQ

    @pl.when(qi == 0)
    def _():
        kaug_ref[:, 0:HEAD_DIM] = ksel_ref[0, 0, 0]
        kaug_ref[:, HEAD_DIM:2 * HEAD_DIM] = oh_ref[...]

    q = q_ref[0, 0].reshape(R, HEAD_DIM)
    kc = kc_ref[0, 0, 0]
    vc = vc_ref[0, 0, 0]
    dn_t = (((1,), (1,)), ((), ()))

    row_t = lax.broadcasted_iota(jnp.int32, (R, TQ), 0) % TQ
    col_t = lax.broadcasted_iota(jnp.int32, (R, TQ), 1)

    nprev = -(-(WIN - 1) // TQ)
    mw = None
    for j in range(nprev + 1):
        cc = qi - nprev + j
        rows = pl.ds(pl.multiple_of(jnp.maximum(cc, 0) * TQ, TQ), TQ)
        s = lax.dot_general(q, kwin_ref[0, 0, 0, rows, :], dn_t, preferred_element_type=F32)
        dist = (nprev - j) * TQ + row_t - col_t
        if j == nprev:
            s = jnp.where(dist >= 0, s, NEG)
        else:
            if (nprev - j) * TQ + TQ - 1 > WIN - 1:
                s = jnp.where(dist <= WIN - 1, s, NEG)
            s = s + jnp.where(cc >= 0, 0.0, NEG)
        w_ref[j] = s
        s0, s1 = _halves(s)
        mw = jnp.maximum(s0, s1) if mw is None else jnp.maximum(mw, jnp.maximum(s0, s1))
    wm_ref[...] = mw

    s = lax.dot_general(q, kc, dn_t, preferred_element_type=F32) * SCALE
    tq = t0 + lax.broadcasted_iota(jnp.int32, (R, ncmp), 0) % TQ
    blk_end = lax.broadcasted_iota(jnp.int32, (R, ncmp), 1) * CMP_STRIDE + (CMP_LEN - 1)
    cmask = blk_end <= tq
    sm = jnp.where(cmask, s, NEG)
    m = jnp.max(sm, axis=1, keepdims=True)
    e = jnp.where(cmask, jnp.exp(sm - m), 0.0)
    den = jnp.sum(e, axis=1, keepdims=True)
    p = e * (1.0 / jnp.maximum(den, 1e-30))
    o_cmp = jnp.dot(p.astype(BF16), vc, preferred_element_type=F32)

    psum = p[0:TQ]
    for r in range(1, B_REP):
        psum = psum + p[r * TQ:(r + 1) * TQ]
    p_hi = psum.astype(BF16)
    p_lo = (psum - p_hi.astype(F32)).astype(BF16)
    cov = cov_ref[...]
    p_slc = (jnp.dot(p_hi, cov, preferred_element_type=F32)
             + jnp.dot(p_lo, cov, preferred_element_type=F32)).T[0:n_slc]
    nblk = lax.broadcasted_iota(jnp.int32, (n_slc, TQ), 0)
    cur = (t0 + lax.broadcasted_iota(jnp.int32, (n_slc, TQ), 1)) // SEL_LEN
    forced = (nblk == 0) | (nblk == cur) | (nblk == cur - 1)
    allowed = nblk <= cur
    score = jnp.where(allowed, jnp.where(forced, FORCE_SCORE, p_slc), -jnp.inf)
    rank = jnp.zeros((n_slc, TQ), F32)
    for mblk in range(n_slc):
        row = score[mblk:mblk + 1, :]
        beats = (row > score) | ((row == score) & (nblk > mblk))
        rank = rank + jnp.where(beats, 1.0, 0.0)
    sel_t = jnp.where((rank < N_SELECT) & allowed, 1.0, 0.0)
    sel_pad = jnp.concatenate([sel_t, jnp.zeros((LANES - n_slc, TQ), F32)], axis=0)
    sel = sel_pad.T
    lane = lax.broadcasted_iota(jnp.int32, (TQ, LANES), 1)
    bias = jnp.where((lane < n_slc) & (sel < 0.5), NEG, 0.0).astype(BF16)
    qaug = jnp.concatenate([q, jnp.concatenate([bias] * B_REP, axis=0)], axis=1)

    rows_d = pl.ds(pl.multiple_of(t0, TQ), TQ)

    def sel_scores(c, nch):
        rows = pl.ds(pl.multiple_of(c * TQ, TQ), nch * TQ)
        s = lax.dot_general(qaug, kaug_ref[rows, :], dn_t, preferred_element_type=F32)
        mx = mp_ref[...]
        for u in range(nch):
            su = s[:, u * TQ:(u + 1) * TQ]
            s_ref[c + u] = su
            s0, s1 = _halves(su)
            mx = jnp.maximum(mx, jnp.maximum(s0, s1))
        mp_ref[...] = mx

    def sel_scores_pair(c2, carry):
        sel_scores(2 * c2, 2)
        return carry

    mp_ref[...] = jnp.full((R, LANES), NEG, F32)
    lax.fori_loop(0, qi // 2, sel_scores_pair, 0)
    pl.when(qi % 2 == 1)(lambda: sel_scores(qi - 1, 1))
    s = lax.dot_general(qaug, kaug_ref[rows_d, :], dn_t, preferred_element_type=F32)
    s = jnp.where(col_t <= row_t, s, NEG)
    s_ref[qi] = s
    s0, s1 = _halves(s)
    m = jnp.max(jnp.maximum(mp_ref[...], jnp.maximum(s0, s1)), axis=1, keepdims=True)
    mp_ref[...] = jnp.broadcast_to(m, (R, LANES))
    lp_ref[...] = jnp.zeros((R, LANES), F32)
    acc_ref[...] = jnp.zeros((R, HEAD_DIM), F32)

    def sel_values(c, nch):
        rows = pl.ds(pl.multiple_of(c * TQ, TQ), nch * TQ)
        mb = mp_ref[...]
        lsum = lp_ref[...]
        ps = []
        for u in range(nch):
            s0, s1 = _halves(s_ref[c + u])
            p0 = jnp.exp2((s0 - mb) * (SCALE * LOG2E))
            p1 = jnp.exp2((s1 - mb) * (SCALE * LOG2E))
            lsum = lsum + (p0 + p1)
            ps += [p0.astype(BF16), p1.astype(BF16)]
        lp_ref[...] = lsum
        acc_ref[...] += jnp.dot(jnp.concatenate(ps, axis=1), vsel_ref[0, 0, 0, rows, :],
                                preferred_element_type=F32)

    def sel_values_pair(c2, carry):
        sel_values(2 * c2, 2)
        return carry

    lax.fori_loop(0, (qi + 1) // 2, sel_values_pair, 0)
    pl.when(qi % 2 == 0)(lambda: sel_values(qi, 1))
    o_sel = acc_ref[...] / jnp.sum(lp_ref[...], axis=1, keepdims=True)

    m = jnp.max(wm_ref[...], axis=1, keepdims=True)
    mb = jnp.broadcast_to(m, (R, LANES))
    lw = jnp.zeros((R, LANES), F32)
    o_win = jnp.zeros((R, HEAD_DIM), F32)
    for j in range(nprev + 1):
        rows = pl.ds(pl.multiple_of(jnp.maximum(qi - nprev + j, 0) * TQ, TQ), TQ)
        v = vwin_ref[0, 0, 0, rows, :]
        s0, s1 = _halves(w_ref[j])
        p0 = jnp.exp2((s0 - mb) * (SCALE * LOG2E))
        p1 = jnp.exp2((s1 - mb) * (SCALE * LOG2E))
        lw = lw + (p0 + p1)
        p = jnp.concatenate([p0.astype(BF16), p1.astype(BF16)], axis=1)
        o_win = o_win + jnp.dot(p, v, preferred_element_type=F32)
    o_win = o_win / jnp.sum(lw, axis=1, keepdims=True)

    gates = _sigmoid(gt_ref[0, 0, 0] + gb_ref[0])
    for r in range(B_REP):
        rs = slice(r * TQ, (r + 1) * TQ)
        o = (gates[:, 3 * r:3 * r + 1] * o_cmp[rs]
             + gates[:, 3 * r + 1:3 * r + 2] * o_sel[rs]
             + gates[:, 3 * r + 2:3 * r + 3] * o_win[rs])
        y_ref[0, r] = (o * _silu(z_ref[0, 0, r])).astype(BF16)


def _attn_b(hb, hf, kvc, gate_b):
    _, B, _, S, _ = hb.shape
    TQ = B_TQ
    n_slc = S // SEL_LEN
    ncmp = S // CMP_STRIDE
    j = np.arange(ncmp)[:, None]
    n = np.arange(LANES)[None, :]
    cover = ((j * CMP_STRIDE < (n + 1) * SEL_LEN) & (j * CMP_STRIDE + CMP_LEN > n * SEL_LEN)
             & (j < ncmp - 1) & (n < n_slc)).astype(np.float32)
    onehot = (np.arange(S)[:, None] // SEL_LEN == np.arange(LANES)[None, :]).astype(np.float32)
    nq = B_REP * B_KV
    kv = lambda s0: pl.BlockSpec((1, 1, 1, S, LANES), lambda b, g, i, s0=s0: (0, b, s0 + g, 0, 0))
    return pl.pallas_call(
        functools.partial(_attn_b_kernel, S=S),
        grid=(B, B_KV, S // TQ),
        in_specs=[
            pl.BlockSpec((1, 1, B_REP, TQ, LANES), lambda b, g, i: (0, b, g, i, 0)),
            kv(nq), kv(nq + 2 * B_KV), kv(nq + B_KV), kv(nq + 3 * B_KV),
            pl.BlockSpec((1, 1, 1, ncmp, LANES), lambda b, g, i: (b, 0, g, 0, 0)),
            pl.BlockSpec((1, 1, 1, ncmp, LANES), lambda b, g, i: (b, 1, g, 0, 0)),
            pl.BlockSpec((1, 1, B_REP, TQ, LANES), lambda b, g, i: (0, b, g, i, 0)),
            pl.BlockSpec((1, 1, 1, TQ, LANES), lambda b, g, i: (0, b, nq + g, i, 0)),
            pl.BlockSpec((1, 1, LANES), lambda b, g, i: (g, 0, 0)),
            pl.BlockSpec((ncmp, LANES), lambda b, g, i: (0, 0)),
            pl.BlockSpec((S, LANES), lambda b, g, i: (0, 0)),
        ],
        out_specs=pl.BlockSpec((1, B_REP, TQ, LANES), lambda b, g, i: (b, g, i, 0)),
        out_shape=jax.ShapeDtypeStruct((B, nq, S, LANES), BF16),
        scratch_shapes=[pltpu.VMEM((S, 2 * HEAD_DIM), BF16),
                        pltpu.VMEM((S // TQ, B_REP * TQ, TQ), F32),
                        pltpu.VMEM((B_REP * TQ, LANES), F32),
                        pltpu.VMEM((B_REP * TQ, LANES), F32),
                        pltpu.VMEM((B_REP * TQ, HEAD_DIM), F32),
                        pltpu.VMEM((-(-(WIN - 1) // TQ) + 1, B_REP * TQ, TQ), F32),
                        pltpu.VMEM((B_REP * TQ, LANES), F32)],
        compiler_params=_cparams("parallel", "parallel", "arbitrary"),
        name="attn_b",
    )(hb, hb, hb, hb, hb, kvc, kvc, hf, hf, gate_b, jnp.asarray(cover, BF16), jnp.asarray(onehot, BF16))


def _mixer_b(x, gain, w_in, gate_b, pe_k, w1_k, w2_k, pe_v, w1_v, w2_v, w_out, tables, final_gain=None):
    B, S, D = x.shape
    bw = D
    kvw = B_KV * HEAD_DIM
    o0, o1, o2 = bw, bw + 6 * kvw, 2 * bw + 6 * kvw
    W = w_in
    kvcol = lambda c: W[:, o0 + c * kvw:o0 + (c + 1) * kvw]
    wg = W[:, o2:].reshape(D, B_KV, B_REP * 3)
    wg = jnp.pad(wg, ((0, 0), (0, 0), (0, LANES - B_REP * 3))).reshape(D, B_KV * LANES)
    gb = jnp.pad(gate_b.reshape(B_KV, 1, B_REP * 3), ((0, 0), (0, 0), (0, LANES - B_REP * 3)))
    w_b = [_rope_layout(W[:, :o0]), _rope_layout(kvcol(2)), _rope_layout(kvcol(4)), kvcol(3), kvcol(5)]
    w_f = [_rope_layout(kvcol(0)), kvcol(1), W[:, o1:o2], wg]
    parts = [((bw + 4 * kvw) // LANES, BF16, (bw + 2 * kvw) // LANES),
             (2 * kvw // LANES, F32, kvw // LANES, CMP_STRIDE),
             (bw // LANES + B_KV, F32, 0)]
    hb, hc, hf = _norm_proj(x, gain, jnp.concatenate(w_b + w_f, axis=1).astype(BF16), parts, d=1, tables=tables)
    w1_kl = jnp.swapaxes(_rope_layout(jnp.swapaxes(w1_k, 1, 2)), 1, 2)
    kvc = _compress(hc[0], _rope_layout(pe_k), w1_kl, _rope_layout(w2_k), pe_v, w1_v, w2_v)
    y = _attn_b(hb, hf, kvc, gb)
    return _out_proj(y, w_out.astype(BF16), x, final_gain)


C_PAD = 8


def _layer_c_kernel(*refs, final):
    if final:
        (x_ref, g_ref, wi_ref, cw_ref, cb_ref, wa_ref, ba_ref, wx_ref, bx_ref, lam_ref, wo_ref, fg_ref,
         o_ref, xn_ref, xpad_ref, z_ref, y_ref, h_ref) = refs
    else:
        (x_ref, g_ref, wi_ref, cw_ref, cb_ref, wa_ref, ba_ref, wx_ref, bx_ref, lam_ref, wo_ref,
         o_ref, xn_ref, xpad_ref, z_ref, y_ref, h_ref) = refs
    tc = x_ref.shape[1]
    cwid = z_ref.shape[1]

    @pl.when(pl.program_id(1) == 0)
    def _():
        xpad_ref[0:C_PAD, :] = jnp.zeros((C_PAD, cwid), F32)
        h_ref[...] = jnp.zeros((1, cwid), F32)

    xf = x_ref[0]
    ms = jnp.mean(xf * xf, axis=-1, keepdims=True)
    xn_ref[...] = (xf * lax.rsqrt(ms + NORM_EPS) * g_ref[...]).astype(BF16)

    def project(dst, col0, lo, hi):
        c = lo
        while c < hi:
            tn = min(PROJ_TN, hi - c)
            dst(c, tn, jnp.dot(xn_ref[...], wi_ref[:, col0 + c:col0 + c + tn], preferred_element_type=F32))
            c += tn

    def put_x(c, tn, v):
        xpad_ref[C_PAD:C_PAD + tc, c:c + tn] = v

    def put_z(c, tn, v):
        z_ref[:, c:c + tn] = v

    project(put_x, 0, 0, cwid)
    project(put_z, cwid, 0, cwid)

    row = lax.broadcasted_iota(jnp.int32, (C_CHUNK, LANES), 0)
    for n in range(cwid // LANES):
        cols = slice(n * LANES, (n + 1) * LANES)
        nlam = -lam_ref[:, cols]
        softplus = jnp.maximum(nlam, 0.0) + jnp.log(1.0 + jnp.exp(-jnp.abs(nlam)))
        h_prev = h_ref[:, cols]
        for c in range(tc // C_CHUNK):
            base = c * C_CHUNK
            xc = jnp.broadcast_to(cb_ref[:, cols], (C_CHUNK, LANES))
            for k in range(CONV_W):
                r0 = base + C_PAD - (CONV_W - 1) + k
                xc = xc + cw_ref[k:k + 1, cols] * xpad_ref[r0:r0 + C_CHUNK, cols]
            xcb = xc.astype(BF16)
            r = _sigmoid(jnp.dot(xcb, wa_ref[n], preferred_element_type=F32) + ba_ref[:, cols])
            i = _sigmoid(jnp.dot(xcb, wx_ref[n], preferred_element_type=F32) + bx_ref[:, cols])
            a = jnp.exp2(r * (softplus * (-LRU_C * LOG2E)))
            b = jnp.sqrt(1.0 - a * a) * i * xc
            k = 1
            while k < C_GROUP:
                keep = row % C_GROUP >= k
                a_s = jnp.where(keep, pltpu.roll(a, k, 0), 1.0)
                b_s = jnp.where(keep, pltpu.roll(b, k, 0), 0.0)
                b = a * b_s + b
                a = a * a_s
                k *= 2
            hs = []
            for g in range(C_CHUNK // C_GROUP):
                rows = slice(g * C_GROUP, (g + 1) * C_GROUP)
                hs.append(b[rows] + a[rows] * h_prev)
                h_prev = hs[-1][C_GROUP - 1:C_GROUP, :]
            h = jnp.concatenate(hs, axis=0)
            y_ref[base:base + C_CHUNK, cols] = (h * _silu(z_ref[base:base + C_CHUNK, cols])).astype(BF16)
        h_ref[:, cols] = h_prev

    xpad_ref[0:C_PAD, :] = xpad_ref[tc:tc + C_PAD, :]
    xo = xf + jnp.dot(y_ref[...], wo_ref[...], preferred_element_type=F32)
    if final:
        ms = jnp.mean(xo * xo, axis=-1, keepdims=True)
        xo = xo * lax.rsqrt(ms + NORM_EPS) * fg_ref[...]
    o_ref[0] = xo


def _mixer_c(x, gain, w_in, conv_w, conv_b, wa, ba, wx, bx, lam, w_out, final_gain=None, tc=512):
    B, S, D = x.shape
    cwid = w_out.shape[0]
    final = final_gain is not None
    const = lambda shape: pl.BlockSpec(shape, lambda b, t: (0,) * len(shape))
    vec = lambda v: v.reshape(1, -1)
    in_specs = [
        pl.BlockSpec((1, tc, D), lambda b, t: (b, t, 0)),
        const((1, D)), const((D, 2 * cwid)), const((CONV_W, cwid)), const((1, cwid)),
        const(wa.shape), const((1, cwid)), const(wx.shape), const((1, cwid)), const((1, cwid)),
        const((cwid, D)),
    ]
    args = [x, vec(gain), w_in.astype(BF16), conv_w, vec(conv_b), wa.astype(BF16), vec(ba), wx.astype(BF16),
            vec(bx), vec(lam), w_out.astype(BF16)]
    if final:
        in_specs.append(const((1, D)))
        args.append(vec(final_gain))
    return pl.pallas_call(
        functools.partial(_layer_c_kernel, final=final),
        grid=(B, S // tc),
        in_specs=in_specs,
        out_specs=pl.BlockSpec((1, tc, D), lambda b, t: (b, t, 0)),
        out_shape=jax.ShapeDtypeStruct((B, S, D), F32),
        scratch_shapes=[pltpu.VMEM((tc, D), BF16), pltpu.VMEM((C_PAD + tc, cwid), F32),
                        pltpu.VMEM((tc, cwid), F32), pltpu.VMEM((tc, cwid), BF16), pltpu.VMEM((1, cwid), F32)],
        compiler_params=_cparams("parallel", "arbitrary"),
        name="layer_c",
    )(*args)


def kernel(x, norm_g, final_g, a_w_in, a_w_out, b_w_in, b_gate_b, b_pe_k, b_w1_k, b_w2_k, b_pe_v, b_w1_v,
           b_w2_v, b_w_out, c_w_in, c_conv_w, c_conv_b, c_wa, c_ba, c_wx, c_bx, c_lambda, c_w_out):
    depth = norm_g.shape[0]
    tables = _rope_tables(x.shape[1])
    for i in range(depth):
        kind, j = i % 3, i // 3
        final = final_g if i == depth - 1 else None
        if kind == 0:
            x = _mixer_a(x, norm_g[i], (a_w_in, j), a_w_out[j], tables, final)
        elif kind == 1:
            x = _mixer_b(x, norm_g[i], b_w_in[j], b_gate_b[j], b_pe_k[j], b_w1_k[j], b_w2_k[j],
                         b_pe_v[j], b_w1_v[j], b_w2_v[j], b_w_out[j], tables, final)
        else:
            x = _mixer_c(x, norm_g[i], c_w_in[j], c_conv_w[j], c_conv_b[j], c_wa[j], c_ba[j],
                         c_wx[j], c_bx[j], c_lambda[j], c_w_out[j], final)
    return x
```

```python
import functools

import numpy as np
import jax
import jax.numpy as jnp
from jax import lax
from jax.experimental import pallas as pl
from jax.experimental.pallas import tpu as pltpu

F32 = jnp.float32
BF16 = jnp.bfloat16

HEAD_DIM = 128
LANES = 128
SUBLANES = 8
ROPE_DIM = HEAD_DIM // 4
ROPE_HALF = ROPE_DIM // 2
ROPE_THETA = 500000.0
NORM_EPS = 1e-6
SCALE = HEAD_DIM ** -0.5
NEG = -1e30
VMEM_LIMIT = 56 * 1024 * 1024

A_GROUPS = ((128, 1), (512, 4), (2048, 16))
A_BLK = 128
A_UNROLL = 16
B_KV = 2
B_REP = 4
CMP_LEN = 32
CMP_STRIDE = 16
SEL_LEN = 64
N_SELECT = 16
WIN = 512
FORCE_SCORE = 1000.0
B_TQ = 256
C_BLOCKS = 10
CONV_W = 4
LRU_C = 8.0
C_CHUNK = 256
C_GROUP = 32


def _cparams(*sem):
    return pltpu.CompilerParams(dimension_semantics=sem, vmem_limit_bytes=VMEM_LIMIT)


LOG2E = 1.4426950408889634


def _sigmoid(x):
    return 1.0 / (1.0 + jnp.exp2(x * -LOG2E))


def _silu(x):
    return x * _sigmoid(x)


ROPE_SWAP = LANES // 2


def _rope_layout(w):
    lead = w.shape[:-1]
    h = w.reshape(*lead, -1, HEAD_DIM)
    h = jnp.concatenate([h[..., :ROPE_HALF], h[..., ROPE_DIM:ROPE_SWAP + ROPE_HALF],
                         h[..., ROPE_HALF:ROPE_DIM], h[..., ROPE_SWAP + ROPE_HALF:]], axis=-1)
    return h.reshape(*lead, -1)


def _rope_tables(S):
    inv_freq = ROPE_THETA ** (-2.0 * np.arange(ROPE_HALF, dtype=np.float64) / ROPE_DIM)
    ang = np.arange(S, dtype=np.float64)[:, None] * inv_freq[None, :]
    cos, sin = np.cos(ang), np.sin(ang)
    gap = ROPE_SWAP - ROPE_HALF
    c = np.concatenate([cos, np.ones((S, gap)), cos, np.ones((S, gap))], axis=1).astype(np.float32)
    sn = np.concatenate([-sin, np.zeros((S, gap)), sin, np.zeros((S, gap))], axis=1).astype(np.float32)
    return c, sn


PROJ_TN = 512


def _dilated_row_copies(x_hbm, xbuf, sem, step, slot, *, d, split, nb, lt, n_mt):
    r = step // n_mt
    mi = step % n_mt
    ls = lt // split
    return [pltpu.make_async_copy(x_hbm.at[mi * nb + bb, :, r + d * j, :],
                                  xbuf.at[slot, pl.ds(bb * lt + j * ls, ls), :], sem.at[slot])
            for bb in range(nb) for j in range(split)]


def _norm_proj_kernel(*refs, nb, lt, parts, any_rope, gather):
    n_out = len(parts)
    split = 1
    if any(p[3] for p in parts):
        cstage = refs[-1]
        refs = refs[:-1]
    if gather:
        d, split, n_mt = gather
        if split > 1:
            stage = refs[-1]
            refs = refs[:-1]
        xbuf, sem = refs[-2:]
        refs = refs[:-2]
        step = pl.program_id(0) * n_mt + pl.program_id(1)
        slot = step % 2
        copies = functools.partial(_dilated_row_copies, refs[0], xbuf, sem, d=d, split=split, nb=nb, lt=lt,
                                   n_mt=n_mt)

        @pl.when(step == 0)
        def _():
            for cp in copies(0, 0):
                cp.start()

        @pl.when(step + 1 < d * n_mt)
        def _():
            for cp in copies(step + 1, 1 - slot):
                cp.start()

        for cp in copies(step, slot):
            cp.wait()
    o_refs, xn_ref = refs[-n_out - 1:-1], refs[-1]
    if any_rope:
        x_ref, g_ref, w_ref, c_ref, sn_ref = refs[:5]
    else:
        x_ref, g_ref, w_ref = refs[:3]
    xf = xbuf[slot] if gather else x_ref[...]
    ms = jnp.mean(xf * xf, axis=-1, keepdims=True)
    xn_ref[...] = (xf * lax.rsqrt(ms + NORM_EPS) * g_ref[...]).astype(BF16)
    col = 0
    for o_ref, (ns, _, n_rope, ck) in zip(o_refs, parts):
        hs = PROJ_TN // LANES if ns % (PROJ_TN // LANES) == 0 else 2
        assert ns % hs == 0
        for c in range(ns // hs):
            tn = hs * LANES
            res = jnp.dot(xn_ref[...], w_ref[:, col:col + tn], preferred_element_type=F32)
            col += tn
            for bb in range(nb):
                for hh in range(hs):
                    sub = res[bb * lt:(bb + 1) * lt, hh * LANES:(hh + 1) * LANES]
                    if c * hs + hh < n_rope:
                        sub = sub * c_ref[...] + pltpu.roll(sub, ROPE_SWAP, 1) * sn_ref[...]
                    if split > 1:
                        ls = lt // split
                        sb = (c * hs + hh) % 2
                        for j in range(split):
                            stage[sb, pl.ds(j, ls, stride=split), :] = sub[j * ls:(j + 1) * ls]
                        sub = stage[sb]
                    if ck:
                        cstage[...] = sub
                        for p in range(ck):
                            o_ref[0, bb, c * hs + hh, :, p * LANES:(p + 1) * LANES] = (
                                cstage[pl.ds(p, lt // ck, stride=ck), :].astype(o_ref.dtype))
                    else:
                        o_ref[0, bb, c * hs + hh] = sub.astype(o_ref.dtype)


def _norm_proj(x, gain, w, parts, *, d, tables, tm=1024):
    B, S, D = x.shape
    N = w.shape[1]
    L = S // d
    rows = B * L
    tm = min(tm, rows)
    assert N == LANES * sum(p[0] for p in parts) and rows % tm == 0
    assert (tm % L == 0) or (L % tm == 0)
    parts = [tuple(p) + (0,) * (4 - len(p)) for p in parts]
    if tm >= L:
        nb, lt = tm // L, L
        o_spec = lambda ns, ck: pl.BlockSpec((1, nb, ns, L // ck, ck * LANES), lambda r, mi: (r, mi, 0, 0, 0))
        t_map = lambda r, mi: (0, r)
    else:
        nb, lt, tps = 1, tm, L // tm
        o_spec = lambda ns, ck: pl.BlockSpec((1, 1, ns, tm // ck, ck * LANES),
                                             lambda r, mi: (r, mi // tps, 0, mi % tps, 0))
        t_map = lambda r, mi: (mi % tps, r)
    split = max(1, SUBLANES // d)
    gather = (d, split, rows // tm) if d > 1 else None
    if gather:
        assert tm >= L and (d * split) % SUBLANES == 0
        x_spec, x_arg = pl.BlockSpec(memory_space=pl.ANY), x.reshape(B, L // split, d * split, D)
        scratch = [pltpu.VMEM((2, tm, D), F32), pltpu.SemaphoreType.DMA((2,))]
        if split > 1:
            scratch.append(pltpu.VMEM((2, lt, LANES), F32))
        sem = ("arbitrary", "arbitrary")
    else:
        x_spec, x_arg = pl.BlockSpec((tm, D), lambda r, mi: (mi, r)), x.reshape(rows, D)
        scratch = []
        sem = ("parallel", "parallel")
    in_specs = [
        x_spec,
        pl.BlockSpec((1, D), lambda r, mi: (0, 0)),
        pl.BlockSpec((D, N), lambda r, mi: (0, 0)),
    ]
    args = [x_arg, gain.reshape(1, D), w]
    any_rope = any(p[2] for p in parts)
    if any_rope:
        in_specs += [pl.BlockSpec((lt, LANES), t_map)] * 2
        for t in tables:
            t = t.reshape(L, d * LANES)
            if gather and split > 1:
                t = np.concatenate([t[j::split] for j in range(split)], axis=0)
            args.append(t)
    if any(p[3] for p in parts):
        scratch.append(pltpu.VMEM((lt, LANES), F32))
    return pl.pallas_call(
        functools.partial(_norm_proj_kernel, nb=nb, lt=lt, parts=tuple(parts), any_rope=any_rope, gather=gather),
        grid=(d, rows // tm),
        in_specs=in_specs,
        out_specs=[o_spec(p[0], max(p[3], 1)) for p in parts],
        out_shape=[jax.ShapeDtypeStruct((d, B, p[0], L // max(p[3], 1), max(p[3], 1) * LANES), p[1])
                   for p in parts],
        scratch_shapes=[pltpu.VMEM((tm, D), BF16)] + scratch,
        compiler_params=_cparams(*sem),
        name=f"norm_proj_d{d}_n{N}",
    )(*args)


def _out_proj_kernel(*refs, nh, final):
    if final:
        y_ref, w_ref, x_ref, g_ref, o_ref = refs
    else:
        y_ref, w_ref, x_ref, o_ref = refs
    y = jnp.concatenate([y_ref[0, h] for h in range(nh)], axis=1)
    xn = x_ref[0] + jnp.dot(y, w_ref[...], preferred_element_type=F32)
    if final:
        ms = jnp.mean(xn * xn, axis=-1, keepdims=True)
        xn = xn * lax.rsqrt(ms + NORM_EPS) * g_ref[...]
    o_ref[0] = xn


def _out_proj(y, w, x, final_gain=None, tm=1024):
    B, nh, S, _ = y.shape
    D = x.shape[-1]
    final = final_gain is not None
    in_specs = [
        pl.BlockSpec((1, nh, tm, LANES), lambda b, i: (b, 0, i, 0)),
        pl.BlockSpec((nh * LANES, D), lambda b, i: (0, 0)),
        pl.BlockSpec((1, tm, D), lambda b, i: (b, i, 0)),
    ]
    args = [y, w, x]
    if final:
        in_specs.append(pl.BlockSpec((1, D), lambda b, i: (0, 0)))
        args.append(final_gain.reshape(1, D))
    return pl.pallas_call(
        functools.partial(_out_proj_kernel, nh=nh, final=final),
        grid=(B, S // tm),
        in_specs=in_specs,
        out_specs=pl.BlockSpec((1, tm, D), lambda b, i: (b, i, 0)),
        out_shape=jax.ShapeDtypeStruct((B, S, D), F32),
        compiler_params=_cparams("parallel", "parallel"),
        name="out_proj",
    )(*args)


def _attn_a_kernel(q1, k1, v1, q2, k2, v2, q3, k3, v3, z_ref, y_ref, o_s, l_s, t_s, b_s, *, S):
    groups = ((q1, k1, v1), (q2, k2, v2), (q3, k3, v3))
    nblk = S // A_BLK
    PS = 4

    a = lax.broadcasted_iota(jnp.int32, (A_BLK, 2 * A_BLK), 0)
    c = lax.broadcasted_iota(jnp.int32, (A_BLK, 2 * A_BLK), 1)
    b_s[0] = jnp.where(c <= a, 0.0, NEG)
    b_s[1] = jnp.where((c >= a) & (c <= a + A_BLK), 0.0, NEG)

    for gi, ((q_ref, k_ref, v_ref), (win, d)) in enumerate(zip(groups, A_GROUPS)):
        L = S // d
        nqb = L // A_BLK
        nk = 2 * A_BLK if nqb > 1 else A_BLK
        assert win // d == A_BLK and d in (1, PS, PS * PS)

        def blk(tt, carry, q_ref=q_ref, k_ref=k_ref, v_ref=v_ref, d=d, nqb=nqb, nk=nk, gi=gi):
            idx, ss, vs = [], [], []
            for u in range(A_UNROLL):
                t = tt * A_UNROLL + u
                r = t // nqb
                i = t % nqb
                ks = jnp.maximum(i - 1, 0) * A_BLK
                q = q_ref[r, 0, 0, pl.ds(pl.multiple_of(i * A_BLK, A_BLK), A_BLK), :]
                k = k_ref[r, 0, 0, pl.ds(pl.multiple_of(ks, A_BLK), nk), :]
                vs.append(v_ref[r, 0, 0, pl.ds(pl.multiple_of(ks, A_BLK), nk), :])
                ss.append(lax.dot_general(q, k, (((1,), (1,)), ((), ())), preferred_element_type=F32))
                idx.append((t, r, i))
            ps, ms, ls = [], [], []
            for u in range(A_UNROLL):
                s = ss[u] * (SCALE * LOG2E) + b_s[jnp.minimum(idx[u][2], 1), :, 0:nk]
                m = jnp.max(s, axis=1, keepdims=True)
                p = jnp.exp2(s - m)
                ls.append(jnp.sum(p, axis=1, keepdims=True))
                ms.append(m)
                ps.append(p.astype(BF16))
            for u in range(A_UNROLL):
                t, r, i = idx[u]
                o = jnp.dot(ps[u], vs[u], preferred_element_type=F32) / ls[u]
                lse = jnp.broadcast_to(ms[u] + jnp.log2(ls[u]), (A_BLK, LANES))
                if d == 1:
                    rows = pl.ds(pl.multiple_of(t * A_BLK, A_BLK), A_BLK)
                    o_s[gi, rows, :] = o
                    l_s[gi, rows, :] = lse
                elif d == PS:
                    rows = pl.ds(i * (A_BLK * d) + r, A_BLK, stride=d)
                    o_s[gi, rows, :] = o
                    l_s[gi, rows, :] = lse
                else:
                    rows = pl.ds(r // PS, A_BLK, stride=PS)
                    t_s[0, r % PS, rows, :] = o
                    t_s[1, r % PS, rows, :] = lse
            return carry

        lax.fori_loop(0, nblk // A_UNROLL, blk, 0)
        if d == PS * PS:
            for r0 in range(PS):
                def second(cidx, carry, r0=r0, gi=gi):
                    src = pl.ds(pl.multiple_of(cidx * A_BLK, A_BLK), A_BLK)
                    dst = pl.ds(cidx * (A_BLK * PS) + r0, A_BLK, stride=PS)
                    o_s[gi, dst, :] = t_s[0, r0, src, :]
                    l_s[gi, dst, :] = t_s[1, r0, src, :]
                    return carry
                lax.fori_loop(0, S // PS // A_BLK, second, 0)

    def merge(c, carry):
        rows = pl.ds(pl.multiple_of(c * A_BLK, A_BLK), A_BLK)
        l1, l2, l3 = l_s[0, rows, :], l_s[1, rows, :], l_s[2, rows, :]
        mx = jnp.maximum(jnp.maximum(l1, l2), l3)
        e1, e2, e3 = jnp.exp2(l1 - mx), jnp.exp2(l2 - mx), jnp.exp2(l3 - mx)
        num = e1 * o_s[0, rows, :] + e2 * o_s[1, rows, :] + e3 * o_s[2, rows, :]
        z = z_ref[0, 0, 0, rows, :]
        y_ref[0, 0, rows, :] = ((num * z) / ((e1 + e2 + e3) * (1.0 + jnp.exp2(z * -LOG2E)))).astype(BF16)
        return carry

    lax.fori_loop(0, nblk, merge, 0, unroll=2)


def _attn_a(qkv, z):
    _, B, nh, S, _ = z.shape
    in_specs, args = [], []
    for arr, (_, d) in zip(qkv, A_GROUPS):
        L = S // d
        for t in range(3):
            in_specs.append(pl.BlockSpec((d, 1, 1, L, LANES), lambda b, h, t=t, nh=nh: (0, b, t * nh + h, 0, 0)))
            args.append(arr)
    in_specs.append(pl.BlockSpec((1, 1, 1, S, LANES), lambda b, h: (0, b, h, 0, 0)))
    args.append(z)
    return pl.pallas_call(
        functools.partial(_attn_a_kernel, S=S),
        grid=(B, nh),
        in_specs=in_specs,
        out_specs=pl.BlockSpec((1, 1, S, LANES), lambda b, h: (b, h, 0, 0)),
        out_shape=jax.ShapeDtypeStruct((B, nh, S, LANES), BF16),
        scratch_shapes=[pltpu.VMEM((3, S, LANES), F32), pltpu.VMEM((3, S, LANES), F32),
                        pltpu.VMEM((2, 4, S // 4, LANES), F32), pltpu.VMEM((2, A_BLK, 2 * A_BLK), F32)],
        compiler_params=_cparams("parallel", "parallel"),
        name="attn_a",
    )(*args)


def _mixer_a(x, gain, w_in, w_out, tables, final_gain=None):
    B, S, D = x.shape
    w_all, j = w_in
    W = lambda c0, c1: w_all[j, :, c0:c1]
    n_g = len(A_GROUPS)
    aw = D
    nh = aw // LANES
    qkv = []
    for g, (_, d) in enumerate(A_GROUPS):
        c0 = g * 3 * aw
        wg = [_rope_layout(W(c0, c0 + 2 * aw)), W(c0 + 2 * aw, c0 + 3 * aw)]
        parts = [(3 * nh, BF16, 2 * nh)]
        if d == 1:
            wg.append(W(n_g * 3 * aw, (n_g * 3 + 1) * aw))
            parts.append((nh, F32, 0))
        outs = _norm_proj(x, gain, jnp.concatenate(wg, axis=1).astype(BF16), parts, d=d, tables=tables)
        qkv.append(outs[0])
        if d == 1:
            z = outs[1]
    y = _attn_a(qkv, z)
    return _out_proj(y, w_out.astype(BF16), x, final_gain)


def _compress_kernel(k_ref, v_ref, pek_ref, pev_ref, w1k_ref, w1v_ref, w2k_ref, w2v_ref, o_ref):
    def one(c_ref, pe_ref, w1_ref, w2_ref):
        half = c_ref.shape[-1]
        ch = c_ref[0, 0]
        u = jnp.dot((ch + pe_ref[0:1, :]).astype(BF16), w1_ref[0:half, :], preferred_element_type=F32)
        v = jnp.dot((ch + pe_ref[1:2, :]).astype(BF16), w1_ref[half:2 * half, :], preferred_element_type=F32)
        h = _silu(u + pltpu.roll(v, v.shape[0] - 1, 0))
        return jnp.dot(h.astype(BF16), w2_ref[...], preferred_element_type=F32).astype(BF16)

    o_ref[0, 0, 0] = one(k_ref, pek_ref, w1k_ref, w2k_ref)
    o_ref[0, 1, 0] = one(v_ref, pev_ref, w1v_ref, w2v_ref)


def _compress(chunks, pe_k, w1_k, w2_k, pe_v, w1_v, w2_v):
    B, _, nch, cw = chunks.shape
    pe = lambda p: p.reshape(2, cw)
    w1 = lambda w: w.reshape(CMP_LEN * HEAD_DIM, -1).astype(BF16)
    const = lambda shape: pl.BlockSpec(shape, lambda b, g: (0,) * len(shape))
    return pl.pallas_call(
        _compress_kernel,
        grid=(B, B_KV),
        in_specs=[
            pl.BlockSpec((1, 1, nch, cw), lambda b, g: (b, g, 0, 0)),
            pl.BlockSpec((1, 1, nch, cw), lambda b, g: (b, B_KV + g, 0, 0)),
            const((2, cw)), const((2, cw)),
            const((CMP_LEN * HEAD_DIM, HEAD_DIM)), const((CMP_LEN * HEAD_DIM, HEAD_DIM)),
            const((HEAD_DIM, HEAD_DIM)), const((HEAD_DIM, HEAD_DIM)),
        ],
        out_specs=pl.BlockSpec((1, 2, 1, nch, HEAD_DIM), lambda b, g: (b, 0, g, 0, 0)),
        out_shape=jax.ShapeDtypeStruct((B, 2, B_KV, nch, HEAD_DIM), BF16),
        compiler_params=_cparams("parallel", "parallel"),
        name="compress",
    )(chunks, chunks, pe(pe_k), pe(pe_v), w1(w1_k), w1(w1_v), w2_k.astype(BF16), w2_v.astype(BF16))


def _halves(x):
    return x[:, 0:LANES], x[:, LANES:2 * LANES]


def _attn_b_kernel(q_ref, ksel_ref, vsel_ref, kwin_ref, vwin_ref, kc_ref, vc_ref, z_ref, gt_ref, gb_ref,
                   cov_ref, oh_ref, y_ref, kaug_ref, s_ref, mp_ref, lp_ref, acc_ref, w_ref, wm_ref, *, S):
    TQ = B_TQ
    R = B_REP * TQ
    n_slc = S // SEL_LEN
    ncmp = S // CMP_STRIDE
    qi = pl.program_id(2)
    t0 = qi * TQ

    @pl.when(qi == 0)
    def _():
        kaug_ref[:, 0:HEAD_DIM] = ksel_ref[0, 0, 0]
        kaug_ref[:, HEAD_DIM:2 * HEAD_DIM] = oh_ref[...]

    q = q_ref[0, 0].reshape(R, HEAD_DIM)
    kc = kc_ref[0, 0, 0]
    vc = vc_ref[0, 0, 0]
    dn_t = (((1,), (1,)), ((), ()))

    row_t = lax.broadcasted_iota(jnp.int32, (R, TQ), 0) % TQ
    col_t = lax.broadcasted_iota(jnp.int32, (R, TQ), 1)

    nprev = -(-(WIN - 1) // TQ)
    mw = None
    for j in range(nprev + 1):
        cc = qi - nprev + j
        rows = pl.ds(pl.multiple_of(jnp.maximum(cc, 0) * TQ, TQ), TQ)
        s = lax.dot_general(q, kwin_ref[0, 0, 0, rows, :], dn_t, preferred_element_type=F32)
        dist = (nprev - j) * TQ + row_t - col_t
        if j == nprev:
            s = jnp.where(dist >= 0, s, NEG)
        else:
            if (nprev - j) * TQ + TQ - 1 > WIN - 1:
                s = jnp.where(dist <= WIN - 1, s, NEG)
            s = s + jnp.where(cc >= 0, 0.0, NEG)
        w_ref[j] = s
        s0, s1 = _halves(s)
        mw = jnp.maximum(s0, s1) if mw is None else jnp.maximum(mw, jnp.maximum(s0, s1))
    wm_ref[...] = mw

    s = lax.dot_general(q, kc, dn_t, preferred_element_type=F32) * SCALE
    tq = t0 + lax.broadcasted_iota(jnp.int32, (R, ncmp), 0) % TQ
    blk_end = lax.broadcasted_iota(jnp.int32, (R, ncmp), 1) * CMP_STRIDE + (CMP_LEN - 1)
    cmask = blk_end <= tq
    sm = jnp.where(cmask, s, NEG)
    m = jnp.max(sm, axis=1, keepdims=True)
    e = jnp.where(cmask, jnp.exp(sm - m), 0.0)
    den = jnp.sum(e, axis=1, keepdims=True)
    p = e * (1.0 / jnp.maximum(den, 1e-30))
    o_cmp = jnp.dot(p.astype(BF16), vc, preferred_element_type=F32)

    psum = p[0:TQ]
    for r in range(1, B_REP):
        psum = psum + p[r * TQ:(r + 1) * TQ]
    p_hi = psum.astype(BF16)
    p_lo = (psum - p_hi.astype(F32)).astype(BF16)
    cov = cov_ref[...]
    p_slc = (jnp.dot(p_hi, cov, preferred_element_type=F32)
             + jnp.dot(p_lo, cov, preferred_element_type=F32)).T[0:n_slc]
    nblk = lax.broadcasted_iota(jnp.int32, (n_slc, TQ), 0)
    cur = (t0 + lax.broadcasted_iota(jnp.int32, (n_slc, TQ), 1)) // SEL_LEN
    forced = (nblk == 0) | (nblk == cur) | (nblk == cur - 1)
    allowed = nblk <= cur
    score = jnp.where(allowed, jnp.where(forced, FORCE_SCORE, p_slc), -jnp.inf)
    rank = jnp.zeros((n_slc, TQ), F32)
    for mblk in range(n_slc):
        row = score[mblk:mblk + 1, :]
        beats = (row > score) | ((row == score) & (nblk > mblk))
        rank = rank + jnp.where(beats, 1.0, 0.0)
    sel_t = jnp.where((rank < N_SELECT) & allowed, 1.0, 0.0)
    sel_pad = jnp.concatenate([sel_t, jnp.zeros((LANES - n_slc, TQ), F32)], axis=0)
    sel = sel_pad.T
    lane = lax.broadcasted_iota(jnp.int32, (TQ, LANES), 1)
    bias = jnp.where((lane < n_slc) & (sel < 0.5), NEG, 0.0).astype(BF16)
    qaug = jnp.concatenate([q, jnp.concatenate([bias] * B_REP, axis=0)], axis=1)

    rows_d = pl.ds(pl.multiple_of(t0, TQ), TQ)

    def sel_scores(c, nch):
        rows = pl.ds(pl.multiple_of(c * TQ, TQ), nch * TQ)
        s = lax.dot_general(qaug, kaug_ref[rows, :], dn_t, preferred_element_type=F32)
        mx = mp_ref[...]
        for u in range(nch):
            su = s[:, u * TQ:(u + 1) * TQ]
            s_ref[c + u] = su
            s0, s1 = _halves(su)
            mx = jnp.maximum(mx, jnp.maximum(s0, s1))
        mp_ref[...] = mx

    def sel_scores_pair(c2, carry):
        sel_scores(2 * c2, 2)
        return carry

    mp_ref[...] = jnp.full((R, LANES), NEG, F32)
    lax.fori_loop(0, qi // 2, sel_scores_pair, 0)
    pl.when(qi % 2 == 1)(lambda: sel_scores(qi - 1, 1))
    s = lax.dot_general(qaug, kaug_ref[rows_d, :], dn_t, preferred_element_type=F32)
    s = jnp.where(col_t <= row_t, s, NEG)
    s_ref[qi] = s
    s0, s1 = _halves(s)
    m = jnp.max(jnp.maximum(mp_ref[...], jnp.maximum(s0, s1)), axis=1, keepdims=True)
    mp_ref[...] = jnp.broadcast_to(m, (R, LANES))
    lp_ref[...] = jnp.zeros((R, LANES), F32)
    acc_ref[...] = jnp.zeros((R, HEAD_DIM), F32)

    def sel_values(c, nch):
        rows = pl.ds(pl.multiple_of(c * TQ, TQ), nch * TQ)
        mb = mp_ref[...]
        lsum = lp_ref[...]
        ps = []
        for u in range(nch):
            s0, s1 = _halves(s_ref[c + u])
            p0 = jnp.exp2((s0 - mb) * (SCALE * LOG2E))
            p1 = jnp.exp2((s1 - mb) * (SCALE * LOG2E))
            lsum = lsum + (p0 + p1)
            ps += [p0.astype(BF16), p1.astype(BF16)]
        lp_ref[...] = lsum
        acc_ref[...] += jnp.dot(jnp.concatenate(ps, axis=1), vsel_ref[0, 0, 0, rows, :],
                                preferred_element_type=F32)

    def sel_values_pair(c2, carry):
        sel_values(2 * c2, 2)
        return carry

    lax.fori_loop(0, (qi + 1) // 2, sel_values_pair, 0)
    pl.when(qi % 2 == 0)(lambda: sel_values(qi, 1))
    o_sel = acc_ref[...] / jnp.sum(lp_ref[...], axis=1, keepdims=True)

    m = jnp.max(wm_ref[...], axis=1, keepdims=True)
    mb = jnp.broadcast_to(m, (R, LANES))
    lw = jnp.zeros((R, LANES), F32)
    o_win = jnp.zeros((R, HEAD_DIM), F32)
    for j in range(nprev + 1):
        rows = pl.ds(pl.multiple_of(jnp.maximum(qi - nprev + j, 0) * TQ, TQ), TQ)
        v = vwin_ref[0, 0, 0, rows, :]
        s0, s1 = _halves(w_ref[j])
        p0 = jnp.exp2((s0 - mb) * (SCALE * LOG2E))
        p1 = jnp.exp2((s1 - mb) * (SCALE * LOG2E))
        lw = lw + (p0 + p1)
        p = jnp.concatenate([p0.astype(BF16), p1.astype(BF16)], axis=1)
        o_win = o_win + jnp.dot(p, v, preferred_element_type=F32)
    o_win = o_win / jnp.sum(lw, axis=1, keepdims=True)

    gates = _sigmoid(gt_ref[0, 0, 0] + gb_ref[0])
    for r in range(B_REP):
        rs = slice(r * TQ, (r + 1) * TQ)
        o = (gates[:, 3 * r:3 * r + 1] * o_cmp[rs]
             + gates[:, 3 * r + 1:3 * r + 2] * o_sel[rs]
             + gates[:, 3 * r + 2:3 * r + 3] * o_win[rs])
        y_ref[0, r] = (o * _silu(z_ref[0, 0, r])).astype(BF16)


def _attn_b(hb, hf, kvc, gate_b):
    _, B, _, S, _ = hb.shape
    TQ = B_TQ
    n_slc = S // SEL_LEN
    ncmp = S // CMP_STRIDE
    j = np.arange(ncmp)[:, None]
    n = np.arange(LANES)[None, :]
    cover = ((j * CMP_STRIDE < (n + 1) * SEL_LEN) & (j * CMP_STRIDE + CMP_LEN > n * SEL_LEN)
             & (j < ncmp - 1) & (n < n_slc)).astype(np.float32)
    onehot = (np.arange(S)[:, None] // SEL_LEN == np.arange(LANES)[None, :]).astype(np.float32)
    nq = B_REP * B_KV
    kv = lambda s0: pl.BlockSpec((1, 1, 1, S, LANES), lambda b, g, i, s0=s0: (0, b, s0 + g, 0, 0))
    return pl.pallas_call(
        functools.partial(_attn_b_kernel, S=S),
        grid=(B, B_KV, S // TQ),
        in_specs=[
            pl.BlockSpec((1, 1, B_REP, TQ, LANES), lambda b, g, i: (0, b, g, i, 0)),
            kv(nq), kv(nq + 2 * B_KV), kv(nq + B_KV), kv(nq + 3 * B_KV),
            pl.BlockSpec((1, 1, 1, ncmp, LANES), lambda b, g, i: (b, 0, g, 0, 0)),
            pl.BlockSpec((1, 1, 1, ncmp, LANES), lambda b, g, i: (b, 1, g, 0, 0)),
            pl.BlockSpec((1, 1, B_REP, TQ, LANES), lambda b, g, i: (0, b, g, i, 0)),
            pl.BlockSpec((1, 1, 1, TQ, LANES), lambda b, g, i: (0, b, nq + g, i, 0)),
            pl.BlockSpec((1, 1, LANES), lambda b, g, i: (g, 0, 0)),
            pl.BlockSpec((ncmp, LANES), lambda b, g, i: (0, 0)),
            pl.BlockSpec((S, LANES), lambda b, g, i: (0, 0)),
        ],
        out_specs=pl.BlockSpec((1, B_REP, TQ, LANES), lambda b, g, i: (b, g, i, 0)),
        out_shape=jax.ShapeDtypeStruct((B, nq, S, LANES), BF16),
        scratch_shapes=[pltpu.VMEM((S, 2 * HEAD_DIM), BF16),
                        pltpu.VMEM((S // TQ, B_REP * TQ, TQ), F32),
                        pltpu.VMEM((B_REP * TQ, LANES), F32),
                        pltpu.VMEM((B_REP * TQ, LANES), F32),
                        pltpu.VMEM((B_REP * TQ, HEAD_DIM), F32),
                        pltpu.VMEM((-(-(WIN - 1) // TQ) + 1, B_REP * TQ, TQ), F32),
                        pltpu.VMEM((B_REP * TQ, LANES), F32)],
        compiler_params=_cparams("parallel", "parallel", "arbitrary"),
        name="attn_b",
    )(hb, hb, hb, hb, hb, kvc, kvc, hf, hf, gate_b, jnp.asarray(cover, BF16), jnp.asarray(onehot, BF16))


def _mixer_b(x, gain, w_in, gate_b, pe_k, w1_k, w2_k, pe_v, w1_v, w2_v, w_out, tables, final_gain=None,
             defer=False):
    B, S, D = x.shape
    bw = D
    kvw = B_KV * HEAD_DIM
    o0, o1, o2 = bw, bw + 6 * kvw, 2 * bw + 6 * kvw
    W = w_in
    kvcol = lambda c: W[:, o0 + c * kvw:o0 + (c + 1) * kvw]
    wg = W[:, o2:].reshape(D, B_KV, B_REP * 3)
    wg = jnp.pad(wg, ((0, 0), (0, 0), (0, LANES - B_REP * 3))).reshape(D, B_KV * LANES)
    gb = jnp.pad(gate_b.reshape(B_KV, 1, B_REP * 3), ((0, 0), (0, 0), (0, LANES - B_REP * 3)))
    w_b = [_rope_layout(W[:, :o0]), _rope_layout(kvcol(2)), _rope_layout(kvcol(4)), kvcol(3), kvcol(5)]
    w_f = [_rope_layout(kvcol(0)), kvcol(1), W[:, o1:o2], wg]
    parts = [((bw + 4 * kvw) // LANES, BF16, (bw + 2 * kvw) // LANES),
             (2 * kvw // LANES, F32, kvw // LANES, CMP_STRIDE),
             (bw // LANES + B_KV, F32, 0)]
    hb, hc, hf = _norm_proj(x, gain, jnp.concatenate(w_b + w_f, axis=1).astype(BF16), parts, d=1, tables=tables)
    w1_kl = jnp.swapaxes(_rope_layout(jnp.swapaxes(w1_k, 1, 2)), 1, 2)
    kvc = _compress(hc[0], _rope_layout(pe_k), w1_kl, _rope_layout(w2_k), pe_v, w1_v, w2_v)
    y = _attn_b(hb, hf, kvc, gb)
    if defer:
        return y, w_out.astype(BF16)
    return _out_proj(y, w_out.astype(BF16), x, final_gain)


C_PAD = 8


def _layer_c_kernel(*refs, final, pending):
    if pending:
        yp_ref, wp_ref = refs[:2]
        refs = refs[2:]
    if final:
        (x_ref, g_ref, wi_ref, cw_ref, cb_ref, wa_ref, ba_ref, wx_ref, bx_ref, lam_ref, wo_ref, fg_ref,
         o_ref, xn_ref, xpad_ref, z_ref, y_ref, h_ref) = refs
    else:
        (x_ref, g_ref, wi_ref, cw_ref, cb_ref, wa_ref, ba_ref, wx_ref, bx_ref, lam_ref, wo_ref,
         o_ref, xn_ref, xpad_ref, z_ref, y_ref, h_ref) = refs
    tc = x_ref.shape[1]
    cwid = z_ref.shape[1]

    @pl.when(pl.program_id(1) == 0)
    def _():
        xpad_ref[0:C_PAD, :] = jnp.zeros((C_PAD, cwid), F32)
        h_ref[...] = jnp.zeros((1, cwid), F32)

    xf = x_ref[0]
    if pending:
        yp = jnp.concatenate([yp_ref[0, h] for h in range(yp_ref.shape[1])], axis=1)
        xf = xf + jnp.dot(yp, wp_ref[...], preferred_element_type=F32)
    ms = jnp.mean(xf * xf, axis=-1, keepdims=True)
    xn_ref[...] = (xf * lax.rsqrt(ms + NORM_EPS) * g_ref[...]).astype(BF16)

    def project(dst, col0, lo, hi):
        c = lo
        while c < hi:
            tn = min(PROJ_TN, hi - c)
            dst(c, tn, jnp.dot(xn_ref[...], wi_ref[:, col0 + c:col0 + c + tn], preferred_element_type=F32))
            c += tn

    def put_x(c, tn, v):
        xpad_ref[C_PAD:C_PAD + tc, c:c + tn] = v

    def put_z(c, tn, v):
        z_ref[:, c:c + tn] = v

    project(put_x, 0, 0, cwid)
    project(put_z, cwid, 0, cwid)

    row = lax.broadcasted_iota(jnp.int32, (C_CHUNK, LANES), 0)
    for n in range(cwid // LANES):
        cols = slice(n * LANES, (n + 1) * LANES)
        nlam = -lam_ref[:, cols]
        softplus = jnp.maximum(nlam, 0.0) + jnp.log(1.0 + jnp.exp(-jnp.abs(nlam)))
        h_prev = h_ref[:, cols]
        for c in range(tc // C_CHUNK):
            base = c * C_CHUNK
            xc = jnp.broadcast_to(cb_ref[:, cols], (C_CHUNK, LANES))
            for k in range(CONV_W):
                r0 = base + C_PAD - (CONV_W - 1) + k
                xc = xc + cw_ref[k:k + 1, cols] * xpad_ref[r0:r0 + C_CHUNK, cols]
            xcb = xc.astype(BF16)
            r = _sigmoid(jnp.dot(xcb, wa_ref[n], preferred_element_type=F32) + ba_ref[:, cols])
            i = _sigmoid(jnp.dot(xcb, wx_ref[n], preferred_element_type=F32) + bx_ref[:, cols])
            a = jnp.exp2(r * (softplus * (-LRU_C * LOG2E)))
            b = jnp.sqrt(1.0 - a * a) * i * xc
            k = 1
            while k < C_GROUP:
                keep = row % C_GROUP >= k
                a_s = jnp.where(keep, pltpu.roll(a, k, 0), 1.0)
                b_s = jnp.where(keep, pltpu.roll(b, k, 0), 0.0)
                b = a * b_s + b
                a = a * a_s
                k *= 2
            hs = []
            for g in range(C_CHUNK // C_GROUP):
                rows = slice(g * C_GROUP, (g + 1) * C_GROUP)
                hs.append(b[rows] + a[rows] * h_prev)
                h_prev = hs[-1][C_GROUP - 1:C_GROUP, :]
            h = jnp.concatenate(hs, axis=0)
            y_ref[base:base + C_CHUNK, cols] = (h * _silu(z_ref[base:base + C_CHUNK, cols])).astype(BF16)
        h_ref[:, cols] = h_prev

    xpad_ref[0:C_PAD, :] = xpad_ref[tc:tc + C_PAD, :]
    xo = xf + jnp.dot(y_ref[...], wo_ref[...], preferred_element_type=F32)
    if final:
        ms = jnp.mean(xo * xo, axis=-1, keepdims=True)
        xo = xo * lax.rsqrt(ms + NORM_EPS) * fg_ref[...]
    o_ref[0] = xo


def _mixer_c(x, gain, w_in, conv_w, conv_b, wa, ba, wx, bx, lam, w_out, final_gain=None, pending=None, tc=512):
    B, S, D = x.shape
    cwid = w_out.shape[0]
    final = final_gain is not None
    const = lambda shape: pl.BlockSpec(shape, lambda b, t: (0,) * len(shape))
    vec = lambda v: v.reshape(1, -1)
    in_specs, args = [], []
    if pending:
        y_prev, w_prev = pending
        in_specs += [pl.BlockSpec((1, y_prev.shape[1], tc, LANES), lambda b, t: (b, 0, t, 0)), const(w_prev.shape)]
        args += [y_prev, w_prev]
    in_specs += [
        pl.BlockSpec((1, tc, D), lambda b, t: (b, t, 0)),
        const((1, D)), const((D, 2 * cwid)), const((CONV_W, cwid)), const((1, cwid)),
        const(wa.shape), const((1, cwid)), const(wx.shape), const((1, cwid)), const((1, cwid)),
        const((cwid, D)),
    ]
    args += [x, vec(gain), w_in.astype(BF16), conv_w, vec(conv_b), wa.astype(BF16), vec(ba), wx.astype(BF16),
             vec(bx), vec(lam), w_out.astype(BF16)]
    if final:
        in_specs.append(const((1, D)))
        args.append(vec(final_gain))
    return pl.pallas_call(
        functools.partial(_layer_c_kernel, final=final, pending=bool(pending)),
        grid=(B, S // tc),
        in_specs=in_specs,
        out_specs=pl.BlockSpec((1, tc, D), lambda b, t: (b, t, 0)),
        out_shape=jax.ShapeDtypeStruct((B, S, D), F32),
        scratch_shapes=[pltpu.VMEM((tc, D), BF16), pltpu.VMEM((C_PAD + tc, cwid), F32),
                        pltpu.VMEM((tc, cwid), F32), pltpu.VMEM((tc, cwid), BF16), pltpu.VMEM((1, cwid), F32)],
        compiler_params=_cparams("parallel", "arbitrary"),
        name="layer_c",
    )(*args)


def kernel(x, norm_g, final_g, a_w_in, a_w_out, b_w_in, b_gate_b, b_pe_k, b_w1_k, b_w2_k, b_pe_v, b_w1_v,
           b_w2_v, b_w_out, c_w_in, c_conv_w, c_conv_b, c_wa, c_ba, c_wx, c_bx, c_lambda, c_w_out):
    depth = norm_g.shape[0]
    tables = _rope_tables(x.shape[1])
    pending = None
    for i in range(depth):
        kind, j = i % 3, i // 3
        final = final_g if i == depth - 1 else None
        if kind == 0:
            x = _mixer_a(x, norm_g[i], (a_w_in, j), a_w_out[j], tables, final)
        elif kind == 1:
            defer = i + 1 < depth
            out = _mixer_b(x, norm_g[i], b_w_in[j], b_gate_b[j], b_pe_k[j], b_w1_k[j], b_w2_k[j],
                           b_pe_v[j], b_w1_v[j], b_w2_v[j], b_w_out[j], tables, final, defer)
            if defer:
                pending = out
            else:
                x = out
        else:
            x = _mixer_c(x, norm_g[i], c_w_in[j], c_conv_w[j], c_conv_b[j], c_wa[j], c_ba[j],
                         c_wx[j], c_bx[j], c_lambda[j], c_w_out[j], final, pending)
            pending = None
    return x
```
